```python
import jax, jax.numpy as jnp
from jax import lax
import numpy as np

D_MODEL = 1024
BATCH = 8
SEQ = 4096
DEPTH = 4

EPS = 1e-6
NEG_INF = -1e30
ROPE_BASE = 10000.0
MEM_LEN = 256
N_BRANCH = 3

RET_HEADS = 4
RET_QK_DIM = 128
RET_V_DIM = 128
RET_CHUNK = 128

MLA_HEADS = 8
MLA_NOPE_DIM = 64
MLA_ROPE_DIM = 32
MLA_V_DIM = 64
MLA_Q_RANK = 256
MLA_KV_RANK = 128
MLA_BLOCK = 128

SWA_HEADS = 8
SWA_KV_HEADS = 2
SWA_HEAD_DIM = 64
SWA_WINDOW = 128

REL_BUCKETS = 32
REL_MAX_DIST = 128

XA_HEADS = 4
XA_HEAD_DIM = 128

PEER_HEADS = 8
PEER_N_KEYS = 128
PEER_N_EXPERTS = PEER_N_KEYS * PEER_N_KEYS
PEER_D_KEY = 128
PEER_TOPK = 16
PEER_TOKEN_BLOCK = 128

IN_SPLITS = (
    RET_HEADS * RET_QK_DIM, RET_HEADS * RET_QK_DIM, RET_HEADS * RET_V_DIM, RET_HEADS * RET_V_DIM,
    MLA_Q_RANK, MLA_KV_RANK, MLA_ROPE_DIM,
    SWA_HEADS * SWA_HEAD_DIM, SWA_KV_HEADS * SWA_HEAD_DIM, SWA_KV_HEADS * SWA_HEAD_DIM,
    N_BRANCH * D_MODEL,
)
W_IN_COLS = sum(IN_SPLITS)

kernel_name = "hybrid_ret_mla_swa_peer_trunk"


def rms_norm(x, g=None):
    x32 = x.astype(jnp.float32)
    y = x32 * lax.rsqrt(jnp.mean(x32 * x32, axis=-1, keepdims=True) + EPS)
    if g is not None:
        y = y * g.astype(jnp.float32)
    return y.astype(x.dtype)


def rope(t, positions):
    half = t.shape[-1] // 2
    inv_freq = jnp.power(ROPE_BASE, -jnp.arange(half, dtype=jnp.float32) / half)
    ang = positions.astype(jnp.float32)[:, None] * inv_freq[None, :]
    cos = jnp.cos(ang)[None, :, None, :]
    sin = jnp.sin(ang)[None, :, None, :]
    t32 = t.astype(jnp.float32)
    t1, t2 = t32[..., :half], t32[..., half:]
    return jnp.concatenate([t1 * cos - t2 * sin, t2 * cos + t1 * sin], axis=-1).astype(t.dtype)


def retention(q, k, v, g, positions):
    out_dtype = q.dtype
    B, S, H, dk = q.shape
    dv = v.shape[-1]
    C = RET_CHUNK
    NC = S // C
    q = rope(q, positions).astype(jnp.float32)
    k = rope(k, positions).astype(jnp.float32) * (dk ** -0.5)
    v = v.astype(jnp.float32)
    log_gamma = jnp.log(1.0 - jnp.exp2(-5.0 - jnp.arange(H, dtype=jnp.float32)))
    qc = q.reshape(B, NC, C, H, dk)
    kc = k.reshape(B, NC, C, H, dk)
    vc = v.reshape(B, NC, C, H, dv)
    idx = jnp.arange(C, dtype=jnp.float32)
    diff = idx[:, None] - idx[None, :]
    intra_decay = jnp.where(diff >= 0, jnp.exp(log_gamma[:, None, None] * jnp.maximum(diff, 0.0)), 0.0)
    scores = jnp.einsum('bnihd,bnjhd->bnhij', qc, kc) * intra_decay
    inner = jnp.einsum('bnhij,bnjhe->bnihe', scores, vc)
    key_w = jnp.exp(log_gamma[None, :] * (C - 1.0 - idx)[:, None])
    chunk_kv = jnp.einsum('bnjhd,jh,bnjhe->nbhde', kc, key_w, vc)
    chunk_decay = jnp.exp(log_gamma * C)[None, :, None, None]

    def step(state, kv):
        return chunk_decay * state + kv, state

    _, prev_state = lax.scan(step, jnp.zeros((B, H, dk, dv), jnp.float32), chunk_kv)
    query_w = jnp.exp(log_gamma[None, :] * (idx + 1.0)[:, None])
    cross = jnp.einsum('bnihd,nbhde->bnihe', qc, prev_state) * query_w[None, None, :, :, None]
    y = rms_norm(inner + cross)
    y = y * jax.nn.silu(g.astype(jnp.float32)).reshape(B, NC, C, H, dv)
    return y.reshape(B, S, H * dv).astype(out_dtype)


def mla(qa, kva, kr, q_norm_g, w_qb, kv_norm_g, w_kvb, positions):
    B, S, _ = qa.shape
    H, NOPE, R, V, BLK = MLA_HEADS, MLA_NOPE_DIM, MLA_ROPE_DIM, MLA_V_DIM, MLA_BLOCK
    q = (rms_norm(qa, q_norm_g) @ w_qb).reshape(B, S, H, NOPE + R)
    q_nope = q[..., :NOPE]
    q_rope = rope(q[..., NOPE:], positions)
    kv = (rms_norm(kva, kv_norm_g) @ w_kvb).reshape(B, S, H, NOPE + V)
    k_nope, v = kv[..., :NOPE], kv[..., NOPE:]
    k_rope = rope(kr[:, :, None, :], positions)[:, :, 0, :]
    scale = (NOPE + R) ** -0.5
    nb = S // BLK
    qn_b = q_nope.reshape(B, nb, BLK, H, NOPE).transpose(1, 0, 2, 3, 4)
    qr_b = q_rope.reshape(B, nb, BLK, H, R).transpose(1, 0, 2, 3, 4)
    key_pos = jnp.arange(S)

    def block(args):
        qn, qr, b = args
        s = jnp.einsum('bqhd,bkhd->bhqk', qn, k_nope) + jnp.einsum('bqhr,bkr->bhqk', qr, k_rope)
        s = s.astype(jnp.float32) * scale
        qpos = b * BLK + jnp.arange(BLK)
        causal = key_pos[None, :] <= qpos[:, None]
        p = jax.nn.softmax(jnp.where(causal, s, NEG_INF), axis=-1)
        return jnp.einsum('bhqk,bkhe->bqhe', p.astype(v.dtype), v)

    out = lax.map(block, (qn_b, qr_b, jnp.arange(nb)))
    return out.transpose(1, 0, 2, 3, 4).reshape(B, S, H * V)


def t5_causal_buckets(rel):
    n = np.maximum(rel, 0)
    max_exact = REL_BUCKETS // 2
    large = max_exact + (np.log(np.maximum(n, 1) / max_exact) / np.log(REL_MAX_DIST / max_exact)
                         * (REL_BUCKETS - max_exact)).astype(np.int32)
    large = np.minimum(large, REL_BUCKETS - 1)
    return np.where(n < max_exact, n, large).astype(np.int32)


def sliding_window_attention(q, k, v, sinks, rel_bias):
    B, S, Hq, d = q.shape
    Hkv = k.shape[2]
    G = Hq // Hkv
    W = SWA_WINDOW
    nb = S // W
    qb = q.reshape(B, nb, W, Hkv, G, d)

    def band(t):
        tb = t.reshape(B, nb, W, Hkv, d)
        prev = jnp.pad(tb, ((0, 0), (1, 0), (0, 0), (0, 0), (0, 0)))[:, :-1]
        return jnp.concatenate([prev, tb], axis=2)

    kb, vb = band(k), band(v)
    s = jnp.einsum('bnqhgd,bnkhd->bnhgqk', qb, kb).astype(jnp.float32) * (d ** -0.5)
    qi = np.arange(W)[:, None] + W
    kj = np.arange(2 * W)[None, :]
    rel = qi - kj
    bias = rel_bias[t5_causal_buckets(rel)]
    bias = bias.transpose(2, 0, 1).reshape(Hkv, G, W, 2 * W).astype(jnp.float32)
    in_window = (rel >= 0) & (rel < W)
    first_block = in_window & (kj >= W)
    valid = jnp.where((jnp.arange(nb) == 0)[:, None, None], first_block, in_window)
    s = jnp.where(valid[None, :, None, None], s + bias, NEG_INF)
    sink = sinks.astype(jnp.float32).reshape(Hkv, G)[None, None, :, :, None, None]
    m = jnp.maximum(jnp.max(s, axis=-1, keepdims=True), sink)
    e = jnp.exp(s - m)
    p = e / (jnp.sum(e, axis=-1, keepdims=True) + jnp.exp(sink - m))
    out = jnp.einsum('bnhgqk,bnkhd->bnqhgd', p.astype(v.dtype), vb)
    return out.reshape(B, S, Hq * d)


def memory_cross_attention(h, mem_n, w_q, w_kv, w_o):
    B, S, _ = h.shape
    M = mem_n.shape[1]
    q = (h @ w_q).reshape(B, S, XA_HEADS, XA_HEAD_DIM)
    kv = (mem_n @ w_kv).reshape(B, M, 2, XA_HEADS, XA_HEAD_DIM)
    k, v = kv[:, :, 0], kv[:, :, 1]
    s = jnp.einsum('bqhd,bkhd->bhqk', q, k).astype(jnp.float32) * (XA_HEAD_DIM ** -0.5)
    p = jax.nn.softmax(s, axis=-1)
    o = jnp.einsum('bhqk,bkhd->bqhd', p.astype(v.dtype), v).reshape(B, S, XA_HEADS * XA_HEAD_DIM)
    return o @ w_o


def peer_ffn(h, w_query, sub_keys, expert_u, expert_v):
    B, S, D = h.shape
    T = B * S
    H, K, NK, Tb = PEER_HEADS, PEER_TOPK, PEER_N_KEYS, PEER_TOKEN_BLOCK
    hf = h.reshape(T, D)
    q = (hf @ w_query).reshape(T, H, 2, PEER_D_KEY // 2)
    s = jnp.einsum('thpc,hpnc->thpn', q, sub_keys).astype(jnp.float32)
    top_s, top_i = lax.top_k(s, K)
    cand_s = (top_s[:, :, 0, :, None] + top_s[:, :, 1, None, :]).reshape(T, H, K * K)
    cand_i = (top_i[:, :, 0, :, None] * NK + top_i[:, :, 1, None, :]).reshape(T, H, K * K)
    best_s, best_pos = lax.top_k(cand_s, K)
    expert_idx = jnp.take_along_axis(cand_i, best_pos, axis=-1)
    gates = jax.nn.softmax(best_s, axis=-1)
    nt = T // Tb

    def block(args):
        xc, ic, gc = args
        u_sel = expert_u[ic]
        a = jnp.einsum('td,thkd->thk', xc, u_sel).astype(jnp.float32)
        act = (jax.nn.gelu(a, approximate=False) * gc).astype(xc.dtype)
        v_sel = expert_v[ic]
        return jnp.einsum('thk,thkd->td', act, v_sel)

    out = lax.map(block, (hf.reshape(nt, Tb, D), expert_idx.reshape(nt, Tb, H, K), gates.reshape(nt, Tb, H, K)))
    return out.reshape(B, S, D)


def setup_inputs(seed: int = 0) -> dict:
    key = jax.random.key(seed)
    ks = jax.random.split(key, 32)
    f32 = jnp.float32
    L, D = DEPTH, D_MODEL

    def nrm(k, shape, scale):
        return jax.random.normal(k, shape, f32) * scale

    def gain(k, shape):
        return 1.0 + 0.02 * jax.random.normal(k, shape, f32)

    ret_w = RET_HEADS * RET_V_DIM
    mla_w = MLA_HEADS * MLA_V_DIM
    swa_w = SWA_HEADS * SWA_HEAD_DIM
    xa_w = XA_HEADS * XA_HEAD_DIM
    return {
        "x": nrm(ks[0], (BATCH, SEQ, D), 1.0),
        "mem": nrm(ks[1], (BATCH, MEM_LEN, D), 1.0),
        "rel_bias": nrm(ks[2], (REL_BUCKETS, SWA_HEADS), 0.5),
        "g_mix": gain(ks[3], (L, D)),
        "w_in": nrm(ks[4], (L, D, W_IN_COLS), D ** -0.5),
        "mla_q_norm": gain(ks[5], (L, MLA_Q_RANK)),
        "w_mla_qb": nrm(ks[6], (L, MLA_Q_RANK, MLA_HEADS * (MLA_NOPE_DIM + MLA_ROPE_DIM)), MLA_Q_RANK ** -0.5),
        "mla_kv_norm": gain(ks[7], (L, MLA_KV_RANK)),
        "w_mla_kvb": nrm(ks[8], (L, MLA_KV_RANK, MLA_HEADS * (MLA_NOPE_DIM + MLA_V_DIM)), MLA_KV_RANK ** -0.5),
        "swa_sinks": nrm(ks[9], (L, SWA_HEADS), 0.5),
        "w_branch_ret": nrm(ks[10], (L, ret_w, D), ret_w ** -0.5),
        "w_branch_mla": nrm(ks[11], (L, mla_w, D), mla_w ** -0.5),
        "w_branch_swa": nrm(ks[12], (L, swa_w, D), swa_w ** -0.5),
        "w_out": nrm(ks[13], (L, D, D), 0.5 * D ** -0.5),
        "g_xattn": gain(ks[14], (L, D)),
        "g_mem": gain(ks[15], (L, D)),
        "w_xq": nrm(ks[16], (L, D, xa_w), D ** -0.5),
        "w_xkv": nrm(ks[17], (L, D, 2 * xa_w), D ** -0.5),
        "w_xo": nrm(ks[18], (L, xa_w, D), 0.5 * xa_w ** -0.5),
        "g_ffn": gain(ks[19], (L, D)),
        "w_peer_query": nrm(ks[20], (L, D, PEER_HEADS * PEER_D_KEY), D ** -0.5),
        "peer_sub_keys": nrm(ks[21], (L, PEER_HEADS, 2, PEER_N_KEYS, PEER_D_KEY // 2), (PEER_D_KEY // 2) ** -0.5),
        "peer_u": nrm(ks[22], (L, PEER_N_EXPERTS, D), D ** -0.5),
        "peer_v": nrm(ks[23], (L, PEER_N_EXPERTS, D), 0.5 * PEER_HEADS ** -0.5),
        "g_final": gain(ks[24], (D,)),
    }


def reference(x, mem, rel_bias, g_mix, w_in, mla_q_norm, w_mla_qb, mla_kv_norm, w_mla_kvb,
              swa_sinks, w_branch_ret, w_branch_mla, w_branch_swa, w_out,
              g_xattn, g_mem, w_xq, w_xkv, w_xo,
              g_ffn, w_peer_query, peer_sub_keys, peer_u, peer_v, g_final):
    B, S, D = x.shape
    positions = jnp.arange(S)
    split_points = [int(p) for p in np.cumsum(IN_SPLITS)[:-1]]
    for l in range(DEPTH):
        h = rms_norm(x, g_mix[l])
        z = h @ w_in[l]
        rq, rk, rv, rg, qa, kva, kr, sq, sk, sv, gate_logits = jnp.split(z, split_points, axis=-1)
        y_ret = retention(rq.reshape(B, S, RET_HEADS, RET_QK_DIM), rk.reshape(B, S, RET_HEADS, RET_QK_DIM),
                          rv.reshape(B, S, RET_HEADS, RET_V_DIM), rg.reshape(B, S, RET_HEADS, RET_V_DIM), positions)
        y_mla = mla(qa, kva, kr, mla_q_norm[l], w_mla_qb[l], mla_kv_norm[l], w_mla_kvb[l], positions)
        y_swa = sliding_window_attention(sq.reshape(B, S, SWA_HEADS, SWA_HEAD_DIM),
                                         sk.reshape(B, S, SWA_KV_HEADS, SWA_HEAD_DIM),
                                         sv.reshape(B, S, SWA_KV_HEADS, SWA_HEAD_DIM),
                                         swa_sinks[l], rel_bias)
        gates = jax.nn.sigmoid(gate_logits.astype(jnp.float32)).astype(x.dtype).reshape(B, S, N_BRANCH, D)
        merged = (gates[:, :, 0] * (y_ret @ w_branch_ret[l])
                  + gates[:, :, 1] * (y_mla @ w_branch_mla[l])
                  + gates[:, :, 2] * (y_swa @ w_branch_swa[l]))
        x = x + merged @ w_out[l]
        x = x + memory_cross_attention(rms_norm(x, g_xattn[l]), rms_norm(mem, g_mem[l]), w_xq[l], w_xkv[l], w_xo[l])
        x = x + peer_ffn(rms_norm(x, g_ffn[l]), w_peer_query[l], peer_sub_keys[l], peer_u[l], peer_v[l])
    return rms_norm(x, g_final)
```

```python
import functools
import math

import numpy as np
import jax
import jax.numpy as jnp
from jax import lax
from jax.experimental import pallas as pl
from jax.experimental.pallas import tpu as pltpu

F32 = jnp.float32
BF16 = jnp.bfloat16

EPS = 1e-6
NEG_INF = -1e30
ROPE_BASE = 10000.0
N_BRANCH = 3

RET_HEADS = 4
RET_DIM = 128
RET_CHUNK = 128

MLA_HEADS = 8
MLA_NOPE = 64
MLA_ROPE = 32
MLA_V = 64
MLA_Q_RANK = 256
MLA_KV_RANK = 128

SWA_HEADS = 8
SWA_KV_HEADS = 2
SWA_DIM = 64
SWA_WINDOW = 128
REL_BUCKETS = 32
REL_MAX_DIST = 128

XA_HEADS = 4
XA_DIM = 128

PEER_HEADS = 8
PEER_KEYS = 128
PEER_DKEY = 128
PEER_TOPK = 16

LANES = 128
HEAD_PAD = LANES

Z_RQ, Z_RK, Z_RV, Z_RG = 0, 512, 1024, 1536
Z_SQ = 2048
Z_GATE = 3072
Z_QA = 6144
Z_SK = 6400
Z_SV = 6656
Z_KVA = 6912
Z_KRP = 7040
Z_KRR = 7168
Z_COLS = 7296

VMEM_LIMIT = 56 * 1024 * 1024


def _params(*sem):
    return pltpu.CompilerParams(dimension_semantics=sem, vmem_limit_bytes=VMEM_LIMIT)


def _rms(x, g=None):
    y = x * lax.rsqrt(jnp.mean(x * x, axis=-1, keepdims=True) + EPS)
    return y if g is None else y * g


def _norm_matmul_kernel(x_ref, g_ref, w_ref, o_ref, h_ref):
    @pl.when(pl.program_id(1) == 0)
    def _():
        h_ref[...] = _rms(x_ref[...].astype(F32), g_ref[...]).astype(BF16)

    o_ref[...] = jnp.dot(h_ref[...], w_ref[...], preferred_element_type=F32).astype(o_ref.dtype)


def norm_matmul(x, g, w, bm, bn, out_dtype):
    m, k = x.shape
    n = w.shape[1]
    return pl.pallas_call(
        _norm_matmul_kernel,
        grid=(m // bm, n // bn),
        in_specs=[pl.BlockSpec((bm, k), lambda i, j: (i, 0)),
                  pl.BlockSpec((1, k), lambda i, j: (0, 0)),
                  pl.BlockSpec((k, bn), lambda i, j: (0, j))],
        out_specs=pl.BlockSpec((bm, bn), lambda i, j: (i, j)),
        out_shape=jax.ShapeDtypeStruct((m, n), out_dtype),
        scratch_shapes=[pltpu.VMEM((bm, k), BF16)],
        compiler_params=_params("parallel", "arbitrary"),
        name="norm_matmul",
    )(x, g.reshape(1, k), w)


def _retention_kernel(q_ref, k_ref, v_ref, g_ref, cos_ref, sin_ref, intra_ref, qw_ref, kw_ref,
                      o_ref, state_ref, *, chunk_decay):
    @pl.when(pl.program_id(1) == 0)
    def _():
        state_ref[...] = jnp.zeros_like(state_ref)

    cos = cos_ref[...]
    sin = sin_ref[...]
    half = RET_DIM // 2

    def rope(t):
        return t * cos + pltpu.roll(t, half, 1) * sin

    for h in range(RET_HEADS):
        cols = slice(h * RET_DIM, (h + 1) * RET_DIM)
        q = rope(q_ref[:, cols].astype(F32)).astype(BF16)
        k = rope(k_ref[:, cols].astype(F32)) * (RET_DIM ** -0.5)
        v = v_ref[:, cols]
        state = state_ref[h]
        scores = lax.dot_general(q, k.astype(BF16), (((1,), (1,)), ((), ())),
                                 preferred_element_type=F32) * intra_ref[h]
        inner = jnp.dot(scores.astype(BF16), v, preferred_element_type=F32)
        cross = jnp.dot(q, state.astype(BF16), preferred_element_type=F32) * qw_ref[h]
        kv = lax.dot_general((k * kw_ref[h]).astype(BF16), v, (((0,), (0,)), ((), ())),
                             preferred_element_type=F32)
        state_ref[h] = chunk_decay[h] * state + kv
        y = _rms(inner + cross)
        gate = g_ref[:, cols].astype(F32)
        o_ref[:, cols] = (y * (gate * jax.nn.sigmoid(gate))).astype(o_ref.dtype)


def retention(z, batch, seq, cos, sin, intra, qw, kw, chunk_decay):
    c = RET_CHUNK
    nc = seq // c
    w = RET_HEADS * RET_DIM
    zspec = lambda col: pl.BlockSpec((c, w), lambda b, n, col=col: (b * nc + n, col // w))
    tab = pl.BlockSpec((c, RET_DIM), lambda b, n: (n, 0))
    const = pl.BlockSpec((RET_HEADS, c, c), lambda b, n: (0, 0, 0))
    return pl.pallas_call(
        functools.partial(_retention_kernel, chunk_decay=chunk_decay),
        grid=(batch, nc),
        in_specs=[zspec(Z_RQ), zspec(Z_RK), zspec(Z_RV), zspec(Z_RG), tab, tab, const, const, const],
        out_specs=pl.BlockSpec((c, w), lambda b, n: (b * nc + n, 0)),
        out_shape=jax.ShapeDtypeStruct((batch * seq, w), BF16),
        scratch_shapes=[pltpu.VMEM((RET_HEADS, RET_DIM, RET_DIM), F32)],
        compiler_params=_params("parallel", "arbitrary"),
        name="retention",
    )(z, z, z, z, cos, sin, intra, qw, kw)


def _mla_q_kernel(qa_ref, g_ref, w1_ref, w2_ref, cos_ref, sin_ref, o_ref):
    qn = _rms(qa_ref[...].astype(F32), g_ref[...]).astype(BF16)
    a = jnp.dot(qn, w1_ref[...], preferred_element_type=F32)
    b = jnp.dot(qn, w2_ref[...], preferred_element_type=F32)
    cos = cos_ref[...]
    sin = sin_ref[...]
    scale = (MLA_NOPE + MLA_ROPE) ** -0.5
    for h in range(MLA_HEADS):
        cols = slice(h * HEAD_PAD, (h + 1) * HEAD_PAD)
        o_ref[:, cols] = ((a[:, cols] * cos + b[:, cols] * sin) * scale).astype(o_ref.dtype)


def mla_q(z, g, w1, w2, cos, sin, seq, bm):
    t = z.shape[0]
    wd = MLA_HEADS * HEAD_PAD
    nsb = seq // bm
    return pl.pallas_call(
        _mla_q_kernel,
        grid=(t // bm,),
        in_specs=[pl.BlockSpec((bm, MLA_Q_RANK), lambda i: (i, Z_QA // MLA_Q_RANK)),
                  pl.BlockSpec((1, MLA_Q_RANK), lambda i: (0, 0)),
                  pl.BlockSpec((MLA_Q_RANK, wd), lambda i: (0, 0)),
                  pl.BlockSpec((MLA_Q_RANK, wd), lambda i: (0, 0)),
                  pl.BlockSpec((bm, HEAD_PAD), lambda i: (i % nsb, 0)),
                  pl.BlockSpec((bm, HEAD_PAD), lambda i: (i % nsb, 0))],
        out_specs=pl.BlockSpec((bm, wd), lambda i: (i, 0)),
        out_shape=jax.ShapeDtypeStruct((t, wd), BF16),
        compiler_params=_params("parallel"),
        name="mla_q",
    )(z, g.reshape(1, -1), w1, w2, cos, sin)


def _mla_kv_kernel(kva_ref, krp_ref, krr_ref, g_ref, wk_ref, wv_ref, cos_ref, sin_ref, k_ref, v_ref):
    kvn = _rms(kva_ref[...].astype(F32), g_ref[...]).astype(BF16)
    k = jnp.dot(kvn, wk_ref[...], preferred_element_type=F32)
    krope = krp_ref[...].astype(F32) * cos_ref[...] + krr_ref[...].astype(F32) * sin_ref[...]
    for h in range(MLA_HEADS):
        cols = slice(h * HEAD_PAD, (h + 1) * HEAD_PAD)
        k_ref[:, cols] = (k[:, cols] + krope).astype(k_ref.dtype)
    v_ref[...] = jnp.dot(kvn, wv_ref[...], preferred_element_type=F32).astype(v_ref.dtype)


def mla_kv(z, g, wk, wv, cos, sin, seq, bm):
    t = z.shape[0]
    wd = MLA_HEADS * HEAD_PAD
    nsb = seq // bm
    zs = lambda col: pl.BlockSpec((bm, LANES), lambda i, col=col: (i, col // LANES))
    tab = pl.BlockSpec((bm, HEAD_PAD), lambda i: (i % nsb, 0))
    wspec = pl.BlockSpec((MLA_KV_RANK, wd), lambda i: (0, 0))
    out = pl.BlockSpec((bm, wd), lambda i: (i, 0))
    return pl.pallas_call(
        _mla_kv_kernel,
        grid=(t // bm,),
        in_specs=[zs(Z_KVA), zs(Z_KRP), zs(Z_KRR), pl.BlockSpec((1, MLA_KV_RANK), lambda i: (0, 0)),
                  wspec, wspec, tab, tab],
        out_specs=[out, out],
        out_shape=[jax.ShapeDtypeStruct((t, wd), BF16)] * 2,
        compiler_params=_params("parallel"),
        name="mla_kv",
    )(z, z, z, g.reshape(1, -1), wk, wv, cos, sin)


def _mla_attn_kernel(q_ref, k_ref, v_ref, o_ref, *, bq, bk):
    qi = pl.program_id(2)
    q = q_ref[...]
    row = qi * bq + lax.broadcasted_iota(jnp.int32, (bq, bk), 0)
    col0 = lax.broadcasted_iota(jnp.int32, (bq, bk), 1)

    def body(j, carry):
        m, l, acc = carry
        start = pl.multiple_of(j * bk, bk)
        kb = k_ref[pl.ds(start, bk), :]
        vb = v_ref[pl.ds(start, bk), :]
        s = lax.dot_general(q, kb, (((1,), (1,)), ((), ())), preferred_element_type=F32)
        s = jnp.where(col0 + j * bk <= row, s, NEG_INF)
        m_new = jnp.maximum(m, jnp.max(s, axis=-1, keepdims=True))
        alpha = jnp.exp(m - m_new)
        p = jnp.exp(s - m_new)
        l = alpha * l + jnp.sum(p, axis=-1, keepdims=True)
        acc = alpha * acc + jnp.dot(p.astype(BF16), vb, preferred_element_type=F32)
        return m_new, l, acc

    init = (jnp.full((bq, 1), NEG_INF, F32), jnp.zeros((bq, 1), F32), jnp.zeros((bq, HEAD_PAD), F32))
    n_kv = (qi * bq + bq + bk - 1) // bk
    _, l, acc = lax.fori_loop(0, n_kv, body, init)
    o_ref[...] = (acc / l).astype(o_ref.dtype)


def mla_attention(q, k, v, batch, seq, bq, bk):
    nq = seq // bq
    qspec = pl.BlockSpec((bq, HEAD_PAD), lambda b, h, i: (b * nq + i, h))
    kvspec = pl.BlockSpec((seq, HEAD_PAD), lambda b, h, i: (b, h))
    return pl.pallas_call(
        functools.partial(_mla_attn_kernel, bq=bq, bk=bk),
        grid=(batch, MLA_HEADS, nq),
        in_specs=[qspec, kvspec, kvspec],
        out_specs=qspec,
        out_shape=jax.ShapeDtypeStruct(q.shape, BF16),
        compiler_params=_params("parallel", "parallel", "arbitrary"),
        name="mla_attention",
    )(q, k, v)


def _swa_kernel(q_ref, kc_ref, kp_ref, vc_ref, vp_ref, bias_ref, sink_ref, o_ref):
    w = SWA_WINDOW
    grp = SWA_HEADS // SWA_KV_HEADS
    first = pl.program_id(1) == 0
    prev_half = lax.broadcasted_iota(jnp.int32, (grp * w, 2 * w), 1) < w
    for g in range(SWA_KV_HEADS):
        kcols = slice(g * HEAD_PAD, (g + 1) * HEAD_PAD)
        kb = jnp.concatenate([kp_ref[:, kcols], kc_ref[:, kcols]], axis=0)
        vb = jnp.concatenate([vp_ref[:, kcols], vc_ref[:, kcols]], axis=0)
        q = jnp.concatenate(
            [q_ref[:, (g * grp + j) * HEAD_PAD:(g * grp + j + 1) * HEAD_PAD] for j in range(grp)], axis=0)
        s = lax.dot_general(q, kb, (((1,), (1,)), ((), ())), preferred_element_type=F32)
        s = s * (SWA_DIM ** -0.5) + bias_ref[g]
        s = jnp.where(jnp.logical_and(first, prev_half), NEG_INF, s)
        sink = sink_ref[g]
        m = jnp.maximum(jnp.max(s, axis=-1, keepdims=True), sink)
        e = jnp.exp(s - m)
        p = e / (jnp.sum(e, axis=-1, keepdims=True) + jnp.exp(sink - m))
        out = jnp.dot(p.astype(BF16), vb, preferred_element_type=F32)
        for j in range(grp):
            o_ref[:, (g * grp + j) * HEAD_PAD:(g * grp + j + 1) * HEAD_PAD] = (
                out[j * w:(j + 1) * w].astype(o_ref.dtype))


def swa(z, batch, seq, bias, sinks):
    w = SWA_WINDOW
    nb = seq // w
    qw = SWA_HEADS * HEAD_PAD
    kw = SWA_KV_HEADS * HEAD_PAD
    grp = SWA_HEADS // SWA_KV_HEADS
    cur = lambda col: pl.BlockSpec((w, kw), lambda b, n, col=col: (b * nb + n, col // kw))
    prev = lambda col: pl.BlockSpec((w, kw), lambda b, n, col=col: (b * nb + jnp.maximum(n - 1, 0), col // kw))
    return pl.pallas_call(
        _swa_kernel,
        grid=(batch, nb),
        in_specs=[pl.BlockSpec((w, qw), lambda b, n: (b * nb + n, Z_SQ // qw)),
                  cur(Z_SK), prev(Z_SK), cur(Z_SV), prev(Z_SV),
                  pl.BlockSpec((SWA_KV_HEADS, grp * w, 2 * w), lambda b, n: (0, 0, 0)),
                  pl.BlockSpec((SWA_KV_HEADS, grp * w, 1), lambda b, n: (0, 0, 0))],
        out_specs=pl.BlockSpec((w, qw), lambda b, n: (b * nb + n, 0)),
        out_shape=jax.ShapeDtypeStruct((batch * seq, qw), BF16),
        compiler_params=_params("parallel", "arbitrary"),
        name="swa",
    )(z, z, z, z, z, bias, sinks)


def _merge_kernel(x_ref, yr_ref, ym_ref, ys_ref, gl_ref, wr_ref, wm_ref, ws_ref, wo_ref, o_ref):
    d = x_ref.shape[1]
    merged = None
    for i, (y_ref, w_ref) in enumerate(((yr_ref, wr_ref), (ym_ref, wm_ref), (ys_ref, ws_ref))):
        gate = jax.nn.sigmoid(gl_ref[:, i * d:(i + 1) * d].astype(F32))
        term = gate * jnp.dot(y_ref[...], w_ref[...], preferred_element_type=F32)
        merged = term if merged is None else merged + term
    o_ref[...] = x_ref[...] + jnp.dot(merged.astype(BF16), wo_ref[...], preferred_element_type=F32)


def merge(x, y_ret, y_mla, y_swa, z, wr, wm, ws, wo, bm):
    t, d = x.shape
    row = lambda width: pl.BlockSpec((bm, width), lambda i: (i, 0))
    full = lambda a: pl.BlockSpec(a.shape, lambda i: (0, 0))
    return pl.pallas_call(
        _merge_kernel,
        grid=(t // bm,),
        in_specs=[row(d), row(y_ret.shape[1]), row(y_mla.shape[1]), row(y_swa.shape[1]),
                  pl.BlockSpec((bm, N_BRANCH * d), lambda i: (i, Z_GATE // (N_BRANCH * d))),
                  full(wr), full(wm), full(ws), full(wo)],
        out_specs=row(d),
        out_shape=jax.ShapeDtypeStruct((t, d), F32),
        compiler_params=_params("parallel"),
        name="merge",
    )(x, y_ret, y_mla, y_swa, z, wr, wm, ws, wo)


def _xattn_kernel(x_ref, g_ref, wq_ref, kv_ref, wo_ref, o_ref):
    x = x_ref[...]
    hn = _rms(x, g_ref[...]).astype(BF16)
    q = (jnp.dot(hn, wq_ref[...], preferred_element_type=F32) * (XA_DIM ** -0.5)).astype(BF16)
    width = XA_HEADS * XA_DIM
    outs = []
    for h in range(XA_HEADS):
        cols = slice(h * XA_DIM, (h + 1) * XA_DIM)
        k = kv_ref[0, :, cols]
        v = kv_ref[0, :, width + h * XA_DIM:width + (h + 1) * XA_DIM]
        s = lax.dot_general(q[:, cols], k, (((1,), (1,)), ((), ())), preferred_element_type=F32)
        e = jnp.exp(s - jnp.max(s, axis=-1, keepdims=True))
        p = e / jnp.sum(e, axis=-1, keepdims=True)
        outs.append(jnp.dot(p.astype(BF16), v, preferred_element_type=F32).astype(BF16))
    o = jnp.concatenate(outs, axis=1)
    o_ref[...] = x + jnp.dot(o, wo_ref[...], preferred_element_type=F32)


def xattn(x, g, wq, memkv, wo, seq, bm):
    t, d = x.shape
    nsb = seq // bm
    full = lambda a: pl.BlockSpec(a.shape, lambda i: (0, 0))
    return pl.pallas_call(
        _xattn_kernel,
        grid=(t // bm,),
        in_specs=[pl.BlockSpec((bm, d), lambda i: (i, 0)),
                  pl.BlockSpec((1, d), lambda i: (0, 0)),
                  full(wq),
                  pl.BlockSpec((1,) + memkv.shape[1:], lambda i: (i // nsb, 0, 0)),
                  full(wo)],
        out_specs=pl.BlockSpec((bm, d), lambda i: (i, 0)),
        out_shape=jax.ShapeDtypeStruct((t, d), F32),
        compiler_params=_params("parallel"),
        name="xattn",
    )(x, g.reshape(1, d), wq, memkv, wo)


_PEER_CAND = [(a, b) for a in range(PEER_TOPK) for b in range(PEER_TOPK) if (a + 1) * (b + 1) <= PEER_TOPK]
_PEER_NCAND = len(_PEER_CAND)
_PEER_CAND_ROWS = -(-_PEER_NCAND // 8) * 8


def _top16_rows(s, vals_ref, idx_ref, payload=None):
    n = s.shape[0]
    rows = lax.broadcasted_iota(jnp.int32, s.shape, 0)
    for r in range(PEER_TOPK):
        m = jnp.max(s, axis=0, keepdims=True)
        win = jnp.min(jnp.where(s == m, rows, n), axis=0, keepdims=True)
        hit = rows == win
        vals_ref[pl.ds(r, 1), :] = m
        if payload is None:
            idx_ref[pl.ds(r, 1), :] = win
        else:
            idx_ref[pl.ds(r, 1), :] = jnp.sum(jnp.where(hit, payload, 0), axis=0, keepdims=True)
        s = jnp.where(hit, -jnp.inf, s)


def _peer_route_kernel(x_ref, g_ref, wq_ref, keys_ref, hn_ref, gate_ref, i1_ref, i2_ref,
                       v1_ref, n1_ref, v2_ref, n2_ref, cs_ref, ci_ref, bs_ref, bi_ref,
                       gt_ref, i1t_ref, i2t_ref):
    bm = x_ref.shape[0]
    hn = _rms(x_ref[...], g_ref[...]).astype(BF16)
    hn_ref[...] = hn
    q = jnp.dot(hn, wq_ref[...], preferred_element_type=F32).astype(BF16)
    k = PEER_TOPK
    cs_ref[...] = jnp.full(cs_ref.shape, -jnp.inf, F32)
    ci_ref[...] = jnp.zeros(ci_ref.shape, jnp.int32)
    for h in range(PEER_HEADS):
        qh = q[:, h * PEER_DKEY:(h + 1) * PEER_DKEY]
        for p, (v_ref, n_ref) in enumerate(((v1_ref, n1_ref), (v2_ref, n2_ref))):
            s = lax.dot_general(keys_ref[2 * h + p], qh, (((1,), (1,)), ((), ())),
                                preferred_element_type=F32)
            _top16_rows(s, v_ref, n_ref)
        for c, (a, b) in enumerate(_PEER_CAND):
            cs_ref[pl.ds(c, 1), :] = v1_ref[pl.ds(a, 1), :] + v2_ref[pl.ds(b, 1), :]
            ci_ref[pl.ds(c, 1), :] = n1_ref[pl.ds(a, 1), :] * PEER_KEYS + n2_ref[pl.ds(b, 1), :]
        _top16_rows(cs_ref[...], bs_ref, bi_ref, payload=ci_ref[...])
        best = bs_ref[...]
        e = jnp.exp(best - best[0:1, :])
        gt_ref[pl.ds(h * k, k), :] = e / jnp.sum(e, axis=0, keepdims=True)
        flat = bi_ref[...]
        i1t_ref[pl.ds(h * k, k), :] = lax.shift_right_logical(flat, int(math.log2(PEER_KEYS)))
        i2t_ref[pl.ds(h * k, k), :] = jnp.bitwise_and(flat, PEER_KEYS - 1)
    gate_ref[...] = gt_ref[...].T
    i1_ref[...] = i1t_ref[...].astype(F32).T.astype(jnp.int32)
    i2_ref[...] = i2t_ref[...].astype(F32).T.astype(jnp.int32)


def peer_route(x, g, wq, keys, bm):
    t, d = x.shape
    slots = PEER_HEADS * PEER_TOPK
    row = lambda width: pl.BlockSpec((bm, width), lambda i: (i, 0))
    tk = lambda dt: pltpu.VMEM((PEER_TOPK, bm), dt)
    return pl.pallas_call(
        _peer_route_kernel,
        grid=(t // bm,),
        in_specs=[row(d), pl.BlockSpec((1, d), lambda i: (0, 0)),
                  pl.BlockSpec(wq.shape, lambda i: (0, 0)),
                  pl.BlockSpec(keys.shape, lambda i: (0, 0, 0))],
        out_specs=[row(d), row(slots), row(slots), row(slots)],
        out_shape=[jax.ShapeDtypeStruct((t, d), BF16),
                   jax.ShapeDtypeStruct((t, slots), F32),
                   jax.ShapeDtypeStruct((t, slots), jnp.int32),
                   jax.ShapeDtypeStruct((t, slots), jnp.int32)],
        scratch_shapes=[tk(F32), tk(jnp.int32), tk(F32), tk(jnp.int32),
                        pltpu.VMEM((_PEER_CAND_ROWS, bm), F32), pltpu.VMEM((_PEER_CAND_ROWS, bm), jnp.int32),
                        tk(F32), tk(jnp.int32),
                        pltpu.VMEM((slots, bm), F32), pltpu.VMEM((slots, bm), jnp.int32),
                        pltpu.VMEM((slots, bm), jnp.int32)],
        compiler_params=_params("parallel"),
        name="peer_route",
    )(x, g.reshape(1, d), wq, keys)


G_PITCH = 136
G_TOKENS_PER_MATMUL = 16


def _peer_gates_kernel(gate_ref, i1_ref, i2_ref, o_ref, s_ref):
    bm = gate_ref.shape[0]
    nk = PEER_KEYS
    slots = gate_ref.shape[2]
    sub = G_TOKENS_PER_MATMUL
    key = lax.broadcasted_iota(jnp.int32, (sub, nk, slots), 1)

    def build(c, carry):
        tok = pl.ds(pl.multiple_of(c * sub, sub), sub)
        wa = jnp.where(i1_ref[tok] == key, gate_ref[tok], 0.0).astype(BF16)
        wb = jnp.where(i2_ref[tok] == key, 1.0, 0.0).astype(BF16)
        s_ref[tok, 0:nk, :] = lax.dot_general(wa, wb, (((2,), (2,)), ((0,), (0,))),
                                              preferred_element_type=F32)
        return carry

    lax.fori_loop(0, bm // sub, build, 0)
    for i in range(nk):
        o_ref[:, i * nk:(i + 1) * nk] = s_ref[:, i, :].astype(o_ref.dtype)


def peer_gates(gate, i1, i2, bm):
    t, slots = gate.shape
    n_exp = PEER_KEYS * PEER_KEYS
    spec = pl.BlockSpec((bm, 1, slots), lambda i: (i, 0, 0))
    r3 = lambda a: a.reshape(t, 1, slots)
    return pl.pallas_call(
        _peer_gates_kernel,
        grid=(t // bm,),
        in_specs=[spec, spec, spec],
        out_specs=pl.BlockSpec((bm, n_exp), lambda i: (i, 0)),
        out_shape=jax.ShapeDtypeStruct((t, n_exp), BF16),
        scratch_shapes=[pltpu.VMEM((bm, G_PITCH, PEER_KEYS), F32)],
        compiler_params=_params("parallel"),
        name="peer_gates",
    )(r3(gate), r3(i1), r3(i2))


def _peer_expert_kernel(x_ref, hn_ref, gm_ref, ut_ref, v_ref, o_ref, acc_ref):
    j = pl.program_id(1)

    @pl.when(j == 0)
    def _():
        acc_ref[...] = jnp.zeros_like(acc_ref)

    a = jnp.dot(hn_ref[...], ut_ref[...], preferred_element_type=F32)
    act = 0.5 * a * (1.0 + lax.erf(a * (2.0 ** -0.5)))
    p = (act * gm_ref[...].astype(F32)).astype(BF16)
    acc_ref[...] += jnp.dot(p, v_ref[...], preferred_element_type=F32)

    @pl.when(j == pl.num_programs(1) - 1)
    def _():
        o_ref[...] = x_ref[...] + acc_ref[...]


def peer_experts(x, hn, gm, ut, v, bm, bn):
    t, d = x.shape
    n_exp = ut.shape[1]
    return pl.pallas_call(
        _peer_expert_kernel,
        grid=(t // bm, n_exp // bn),
        in_specs=[pl.BlockSpec((bm, d), lambda i, j: (i, 0)),
                  pl.BlockSpec((bm, d), lambda i, j: (i, 0)),
                  pl.BlockSpec((bm, bn), lambda i, j: (i, j)),
                  pl.BlockSpec((d, bn), lambda i, j: (0, j)),
                  pl.BlockSpec((bn, d), lambda i, j: (j, 0))],
        out_specs=pl.BlockSpec((bm, d), lambda i, j: (i, 0)),
        out_shape=jax.ShapeDtypeStruct((t, d), F32),
        scratch_shapes=[pltpu.VMEM((bm, d), F32)],
        compiler_params=_params("parallel", "arbitrary"),
        name="peer_experts",
    )(x, hn, gm, ut, v)


def _final_norm_kernel(x_ref, g_ref, o_ref):
    o_ref[...] = _rms(x_ref[...], g_ref[...])


def final_norm(x, g, bm):
    t, d = x.shape
    return pl.pallas_call(
        _final_norm_kernel,
        grid=(t // bm,),
        in_specs=[pl.BlockSpec((bm, d), lambda i: (i, 0)), pl.BlockSpec((1, d), lambda i: (0, 0))],
        out_specs=pl.BlockSpec((bm, d), lambda i: (i, 0)),
        out_shape=jax.ShapeDtypeStruct((t, d), F32),
        compiler_params=_params("parallel"),
        name="final_norm",
    )(x, g.reshape(1, d))


def _pad_heads(w, heads, dim, axis):
    shape = w.shape[:axis] + (heads, dim) + w.shape[axis + 1:]
    w = w.reshape(shape)
    pad = [(0, 0)] * w.ndim
    pad[axis + 1] = (0, HEAD_PAD - dim)
    w = jnp.pad(w, pad)
    return w.reshape(w.shape[:axis] + (heads * HEAD_PAD,) + w.shape[axis + 2:])


def _rot_half_cols(w):
    half = w.shape[-1] // 2
    return jnp.concatenate([-w[..., half:], w[..., :half]], axis=-1)


def _pack_w_in(w_in):
    rq, rk, rv, rg, qa, kva, kr, sq, sk, sv, gate = jnp.split(
        w_in, [int(p) for p in np.cumsum(
            [512, 512, 512, 512, MLA_Q_RANK, MLA_KV_RANK, MLA_ROPE,
             SWA_HEADS * SWA_DIM, SWA_KV_HEADS * SWA_DIM, SWA_KV_HEADS * SWA_DIM])], axis=-1)
    place = lambda w: jnp.pad(w, ((0, 0), (0, 0), (MLA_NOPE, HEAD_PAD - MLA_NOPE - MLA_ROPE)))
    cols = [rq, rk, rv, rg, _pad_heads(sq, SWA_HEADS, SWA_DIM, 2), gate, qa,
            _pad_heads(sk, SWA_KV_HEADS, SWA_DIM, 2), _pad_heads(sv, SWA_KV_HEADS, SWA_DIM, 2),
            kva, place(kr), place(_rot_half_cols(kr))]
    out = jnp.concatenate(cols, axis=-1).astype(BF16)
    assert out.shape[-1] == Z_COLS
    return out


def _rope_tables(seq, half):
    inv_freq = jnp.power(ROPE_BASE, -jnp.arange(half, dtype=F32) / half)
    ang = jnp.arange(seq, dtype=F32)[:, None] * inv_freq[None, :]
    return jnp.cos(ang), jnp.sin(ang)


def _t5_buckets(rel):
    n = np.maximum(rel, 0)
    max_exact = REL_BUCKETS // 2
    large = max_exact + (np.log(np.maximum(n, 1) / max_exact) / np.log(REL_MAX_DIST / max_exact)
                         * (REL_BUCKETS - max_exact)).astype(np.int32)
    large = np.minimum(large, REL_BUCKETS - 1)
    return np.where(n < max_exact, n, large).astype(np.int32)


def _swa_bias(rel_bias):
    w = SWA_WINDOW
    rel = (np.arange(w)[:, None] + w) - np.arange(2 * w)[None, :]
    bias = rel_bias[_t5_buckets(rel)].astype(F32)
    bias = jnp.where(((rel >= 0) & (rel < w))[:, :, None], bias, NEG_INF)
    bias = bias.transpose(2, 0, 1)
    return bias.reshape(SWA_KV_HEADS, (SWA_HEADS // SWA_KV_HEADS) * w, 2 * w)


def _retention_tables():
    c = RET_CHUNK
    log_gamma = np.log(1.0 - np.exp2(-5.0 - np.arange(RET_HEADS, dtype=np.float64)))
    idx = np.arange(c, dtype=np.float64)
    diff = idx[:, None] - idx[None, :]
    intra = np.where(diff >= 0, np.exp(log_gamma[:, None, None] * np.maximum(diff, 0.0)), 0.0)
    qw = np.exp(log_gamma[:, None] * (idx + 1.0)[None, :])
    kw = np.exp(log_gamma[:, None] * (c - 1.0 - idx)[None, :])
    bc = lambda a: jnp.asarray(np.broadcast_to(a[:, :, None], (RET_HEADS, c, c)), F32)
    chunk_decay = tuple(float(v) for v in np.exp(log_gamma * c))
    return jnp.asarray(intra, F32), bc(qw), bc(kw), chunk_decay


def kernel(x, mem, rel_bias, g_mix, w_in, mla_q_norm, w_mla_qb, mla_kv_norm, w_mla_kvb, swa_sinks,
           w_branch_ret, w_branch_mla, w_branch_swa, w_out, g_xattn, g_mem, w_xq, w_xkv, w_xo,
           g_ffn, w_peer_query, peer_sub_keys, peer_u, peer_v, g_final):
    batch, seq, d = x.shape
    depth = w_in.shape[0]
    t = batch * seq
    mem_len = mem.shape[1]
    bm = min(512, seq)

    w_in_p = _pack_w_in(w_in)
    qb = w_mla_qb.reshape(depth, MLA_Q_RANK, MLA_HEADS, MLA_NOPE + MLA_ROPE)
    q_nope, q_rope = qb[..., :MLA_NOPE], qb[..., MLA_NOPE:]
    zpad = jnp.zeros(qb.shape[:3] + (HEAD_PAD - MLA_NOPE - MLA_ROPE,), F32)
    flat = lambda w: w.reshape(w.shape[:2] + (MLA_HEADS * HEAD_PAD,)).astype(BF16)
    w_q1 = flat(jnp.concatenate([q_nope, q_rope, zpad], axis=-1))
    w_q2 = flat(jnp.concatenate([jnp.zeros_like(q_nope), _rot_half_cols(q_rope), zpad], axis=-1))
    kvb = w_mla_kvb.reshape(depth, MLA_KV_RANK, MLA_HEADS, MLA_NOPE + MLA_V)
    w_k = flat(jnp.pad(kvb[..., :MLA_NOPE], ((0, 0),) * 3 + ((0, HEAD_PAD - MLA_NOPE),)))
    w_v = flat(jnp.pad(kvb[..., MLA_NOPE:], ((0, 0),) * 3 + ((0, HEAD_PAD - MLA_V),)))
    w_bret = w_branch_ret.astype(BF16)
    w_bmla = _pad_heads(w_branch_mla, MLA_HEADS, MLA_V, 1).astype(BF16)
    w_bswa = _pad_heads(w_branch_swa, SWA_HEADS, SWA_DIM, 1).astype(BF16)
    w_out_b = w_out.astype(BF16)
    w_xq_b, w_xkv_b, w_xo_b = w_xq.astype(BF16), w_xkv.astype(BF16), w_xo.astype(BF16)
    w_pq = w_peer_query.astype(BF16)
    half = PEER_DKEY // 2
    keys = jnp.stack([jnp.pad(peer_sub_keys[:, :, 0], ((0, 0),) * 3 + ((0, half),)),
                      jnp.pad(peer_sub_keys[:, :, 1], ((0, 0),) * 3 + ((half, 0),))], axis=2)
    keys = keys.reshape(depth, 2 * PEER_HEADS, PEER_KEYS, PEER_DKEY).astype(BF16)
    peer_ut = jnp.swapaxes(peer_u, 1, 2).astype(BF16)
    peer_vb = peer_v.astype(BF16)

    cos64, sin64 = _rope_tables(seq, RET_DIM // 2)
    ret_cos = jnp.concatenate([cos64, cos64], axis=1)
    ret_sin = jnp.concatenate([-sin64, sin64], axis=1)
    cos16, sin16 = _rope_tables(seq, MLA_ROPE // 2)
    tail = jnp.zeros((seq, HEAD_PAD - MLA_NOPE - MLA_ROPE), F32)
    mla_cos = jnp.concatenate([jnp.ones((seq, MLA_NOPE), F32), cos16, cos16, tail], axis=1)
    mla_sin = jnp.concatenate([jnp.zeros((seq, MLA_NOPE), F32), sin16, sin16, tail], axis=1)
    intra, ret_qw, ret_kw, chunk_decay = _retention_tables()
    swa_bias = _swa_bias(rel_bias)
    grp = SWA_HEADS // SWA_KV_HEADS

    x = x.reshape(t, d)
    mem2 = mem.reshape(batch * mem_len, d)
    for l in range(depth):
        z = norm_matmul(x, g_mix[l], w_in_p[l], bm=bm, bn=Z_COLS // 3, out_dtype=BF16)
        y_ret = retention(z, batch, seq, ret_cos, ret_sin, intra, ret_qw, ret_kw, chunk_decay)
        q = mla_q(z, mla_q_norm[l], w_q1[l], w_q2[l], mla_cos, mla_sin, seq, bm)
        k, v = mla_kv(z, mla_kv_norm[l], w_k[l], w_v[l], mla_cos, mla_sin, seq, bm)
        y_mla = mla_attention(q, k, v, batch, seq, bq=bm, bk=bm)
        sinks = jnp.broadcast_to(swa_sinks[l].reshape(SWA_KV_HEADS, grp, 1, 1),
                                 (SWA_KV_HEADS, grp, SWA_WINDOW, 1)).reshape(SWA_KV_HEADS, grp * SWA_WINDOW, 1)
        y_swa = swa(z, batch, seq, swa_bias, sinks)
        x = merge(x, y_ret, y_mla, y_swa, z, w_bret[l], w_bmla[l], w_bswa[l], w_out_b[l], bm)

        memkv = norm_matmul(mem2, g_mem[l], w_xkv_b[l], bm=min(512, batch * mem_len),
                            bn=w_xkv_b.shape[2], out_dtype=BF16).reshape(batch, mem_len, -1)
        x = xattn(x, g_xattn[l], w_xq_b[l], memkv, w_xo_b[l], seq, bm)

        hn, gate, i1, i2 = peer_route(x, g_ffn[l], w_pq[l], keys[l], bm=min(256, seq))
        gm = peer_gates(gate, i1, i2, bm=min(128, seq))
        x = peer_experts(x, hn, gm, peer_ut[l], peer_vb[l], bm=bm, bn=1024)
    return final_norm(x, g_final, bm).reshape(batch, seq, d)
```

```python
import functools
import math

import numpy as np
import jax
import jax.numpy as jnp
from jax import lax
from jax.experimental import pallas as pl
from jax.experimental.pallas import tpu as pltpu

F32 = jnp.float32
BF16 = jnp.bfloat16

EPS = 1e-6
NEG_INF = -1e30
ROPE_BASE = 10000.0
N_BRANCH = 3

RET_HEADS = 4
RET_DIM = 128
RET_CHUNK = 128

MLA_HEADS = 8
MLA_NOPE = 64
MLA_ROPE = 32
MLA_V = 64
MLA_Q_RANK = 256
MLA_KV_RANK = 128

SWA_HEADS = 8
SWA_KV_HEADS = 2
SWA_DIM = 64
SWA_WINDOW = 128
REL_BUCKETS = 32
REL_MAX_DIST = 128

XA_HEADS = 4
XA_DIM = 128

PEER_HEADS = 8
PEER_KEYS = 128
PEER_DKEY = 128
PEER_TOPK = 16

LANES = 128
HEAD_PAD = LANES

Z_RQ, Z_RK, Z_RV, Z_RG = 0, 512, 1024, 1536
Z_SQ = 2048
Z_GATE = 3072
Z_QA = 6144
Z_SK = 6400
Z_SV = 6656
Z_KVA = 6912
Z_KRP = 7040
Z_KRR = 7168
Z_COLS = 7296

VMEM_LIMIT = 56 * 1024 * 1024


def _params(*sem):
    return pltpu.CompilerParams(dimension_semantics=sem, vmem_limit_bytes=VMEM_LIMIT)


def _rms(x, g=None):
    y = x * lax.rsqrt(jnp.mean(x * x, axis=-1, keepdims=True) + EPS)
    return y if g is None else y * g


def _norm_matmul_kernel(x_ref, g_ref, w_ref, o_ref, h_ref):
    @pl.when(pl.program_id(1) == 0)
    def _():
        h_ref[...] = _rms(x_ref[...].astype(F32), g_ref[...]).astype(BF16)

    o_ref[...] = jnp.dot(h_ref[...], w_ref[...], preferred_element_type=F32).astype(o_ref.dtype)


def norm_matmul(x, g, w, bm, bn, out_dtype):
    m, k = x.shape
    n = w.shape[1]
    return pl.pallas_call(
        _norm_matmul_kernel,
        grid=(m // bm, n // bn),
        in_specs=[pl.BlockSpec((bm, k), lambda i, j: (i, 0)),
                  pl.BlockSpec((1, k), lambda i, j: (0, 0)),
                  pl.BlockSpec((k, bn), lambda i, j: (0, j))],
        out_specs=pl.BlockSpec((bm, bn), lambda i, j: (i, j)),
        out_shape=jax.ShapeDtypeStruct((m, n), out_dtype),
        scratch_shapes=[pltpu.VMEM((bm, k), BF16)],
        compiler_params=_params("parallel", "arbitrary"),
        name="norm_matmul",
    )(x, g.reshape(1, k), w)


def _retention_kernel(q_ref, k_ref, v_ref, g_ref, cos_ref, sin_ref, intra_ref, qw_ref, kw_ref,
                      o_ref, state_ref, *, chunk_decay):
    @pl.when(pl.program_id(1) == 0)
    def _():
        state_ref[...] = jnp.zeros_like(state_ref)

    cos = cos_ref[...]
    sin = sin_ref[...]
    half = RET_DIM // 2

    def rope(t):
        return t * cos + pltpu.roll(t, half, 1) * sin

    for h in range(RET_HEADS):
        cols = slice(h * RET_DIM, (h + 1) * RET_DIM)
        q = rope(q_ref[:, cols].astype(F32)).astype(BF16)
        k = rope(k_ref[:, cols].astype(F32)) * (RET_DIM ** -0.5)
        v = v_ref[:, cols]
        state = state_ref[h]
        scores = lax.dot_general(q, k.astype(BF16), (((1,), (1,)), ((), ())),
                                 preferred_element_type=F32) * intra_ref[h]
        inner = jnp.dot(scores.astype(BF16), v, preferred_element_type=F32)
        cross = jnp.dot(q, state.astype(BF16), preferred_element_type=F32) * qw_ref[h]
        kv = lax.dot_general((k * kw_ref[h]).astype(BF16), v, (((0,), (0,)), ((), ())),
                             preferred_element_type=F32)
        state_ref[h] = chunk_decay[h] * state + kv
        y = _rms(inner + cross)
        gate = g_ref[:, cols].astype(F32)
        o_ref[:, cols] = (y * (gate * jax.nn.sigmoid(gate))).astype(o_ref.dtype)


def retention(z, batch, seq, cos, sin, intra, qw, kw, chunk_decay):
    c = RET_CHUNK
    nc = seq // c
    w = RET_HEADS * RET_DIM
    zspec = lambda col: pl.BlockSpec((c, w), lambda b, n, col=col: (b * nc + n, col // w))
    tab = pl.BlockSpec((c, RET_DIM), lambda b, n: (n, 0))
    const = pl.BlockSpec((RET_HEADS, c, c), lambda b, n: (0, 0, 0))
    return pl.pallas_call(
        functools.partial(_retention_kernel, chunk_decay=chunk_decay),
        grid=(batch, nc),
        in_specs=[zspec(Z_RQ), zspec(Z_RK), zspec(Z_RV), zspec(Z_RG), tab, tab, const, const, const],
        out_specs=pl.BlockSpec((c, w), lambda b, n: (b * nc + n, 0)),
        out_shape=jax.ShapeDtypeStruct((batch * seq, w), BF16),
        scratch_shapes=[pltpu.VMEM((RET_HEADS, RET_DIM, RET_DIM), F32)],
        compiler_params=_params("parallel", "arbitrary"),
        name="retention",
    )(z, z, z, z, cos, sin, intra, qw, kw)


def _mla_q_kernel(qa_ref, g_ref, w1_ref, w2_ref, cos_ref, sin_ref, o_ref):
    qn = _rms(qa_ref[...].astype(F32), g_ref[...]).astype(BF16)
    a = jnp.dot(qn, w1_ref[...], preferred_element_type=F32)
    b = jnp.dot(qn, w2_ref[...], preferred_element_type=F32)
    cos = cos_ref[...]
    sin = sin_ref[...]
    scale = (MLA_NOPE + MLA_ROPE) ** -0.5 * math.log2(math.e)
    for h in range(MLA_HEADS):
        cols = slice(h * HEAD_PAD, (h + 1) * HEAD_PAD)
        o_ref[:, cols] = ((a[:, cols] * cos + b[:, cols] * sin) * scale).astype(o_ref.dtype)


def mla_q(z, g, w1, w2, cos, sin, seq, bm):
    t = z.shape[0]
    wd = MLA_HEADS * HEAD_PAD
    nsb = seq // bm
    return pl.pallas_call(
        _mla_q_kernel,
        grid=(t // bm,),
        in_specs=[pl.BlockSpec((bm, MLA_Q_RANK), lambda i: (i, Z_QA // MLA_Q_RANK)),
                  pl.BlockSpec((1, MLA_Q_RANK), lambda i: (0, 0)),
                  pl.BlockSpec((MLA_Q_RANK, wd), lambda i: (0, 0)),
                  pl.BlockSpec((MLA_Q_RANK, wd), lambda i: (0, 0)),
                  pl.BlockSpec((bm, HEAD_PAD), lambda i: (i % nsb, 0)),
                  pl.BlockSpec((bm, HEAD_PAD), lambda i: (i % nsb, 0))],
        out_specs=pl.BlockSpec((bm, wd), lambda i: (i, 0)),
        out_shape=jax.ShapeDtypeStruct((t, wd), BF16),
        compiler_params=_params("parallel"),
        name="mla_q",
    )(z, g.reshape(1, -1), w1, w2, cos, sin)


def _mla_kv_kernel(kva_ref, krp_ref, krr_ref, g_ref, wk_ref, wv_ref, cos_ref, sin_ref, k_ref, v_ref):
    kvn = _rms(kva_ref[...].astype(F32), g_ref[...]).astype(BF16)
    k = jnp.dot(kvn, wk_ref[...], preferred_element_type=F32)
    krope = krp_ref[...].astype(F32) * cos_ref[...] + krr_ref[...].astype(F32) * sin_ref[...]
    for h in range(MLA_HEADS):
        cols = slice(h * HEAD_PAD, (h + 1) * HEAD_PAD)
        k_ref[:, cols] = (k[:, cols] + krope).astype(k_ref.dtype)
    v_ref[...] = jnp.dot(kvn, wv_ref[...], preferred_element_type=F32).astype(v_ref.dtype)


def mla_kv(z, g, wk, wv, cos, sin, seq, bm):
    t = z.shape[0]
    wd = MLA_HEADS * HEAD_PAD
    nsb = seq // bm
    zs = lambda col: pl.BlockSpec((bm, LANES), lambda i, col=col: (i, col // LANES))
    tab = pl.BlockSpec((bm, HEAD_PAD), lambda i: (i % nsb, 0))
    wspec = pl.BlockSpec((MLA_KV_RANK, wd), lambda i: (0, 0))
    out = pl.BlockSpec((bm, wd), lambda i: (i, 0))
    return pl.pallas_call(
        _mla_kv_kernel,
        grid=(t // bm,),
        in_specs=[zs(Z_KVA), zs(Z_KRP), zs(Z_KRR), pl.BlockSpec((1, MLA_KV_RANK), lambda i: (0, 0)),
                  wspec, wspec, tab, tab],
        out_specs=[out, out],
        out_shape=[jax.ShapeDtypeStruct((t, wd), BF16)] * 2,
        compiler_params=_params("parallel"),
        name="mla_kv",
    )(z, z, z, g.reshape(1, -1), wk, wv, cos, sin)


def _mla_attn_kernel(q_ref, k_ref, v_ref, o_ref, *, bq, n_split, heads):
    qi = pl.program_id(2)
    rows = bq // n_split
    chains = [(h, r) for h in range(heads) for r in range(n_split)]
    cols = lambda h: slice(h * HEAD_PAD, (h + 1) * HEAD_PAD)
    qs = [q_ref[pl.ds(r * rows, rows), cols(h)] for h, r in chains]

    def step(j, carry, diagonal):
        start = pl.multiple_of(j * bq, bq)
        out = []
        for c, (h, r) in enumerate(chains):
            m, l, acc = carry[c]
            kb = k_ref[pl.ds(start, bq), cols(h)]
            vb = v_ref[pl.ds(start, bq), cols(h)]
            s = lax.dot_general(qs[c], kb, (((1,), (1,)), ((), ())), preferred_element_type=F32)
            if diagonal:
                row = r * rows + lax.broadcasted_iota(jnp.int32, (rows, bq), 0)
                col = lax.broadcasted_iota(jnp.int32, (rows, bq), 1)
                s = jnp.where(col <= row, s, NEG_INF)
            m_new = jnp.maximum(m, jnp.max(s, axis=-1, keepdims=True))
            alpha = jnp.exp2(m - m_new)
            p = jnp.exp2(s - m_new)
            l = alpha * l + jnp.sum(p, axis=-1, keepdims=True)
            acc = alpha * acc + jnp.dot(p.astype(BF16), vb, preferred_element_type=F32)
            out.append((m_new, l, acc))
        return tuple(out)

    init = tuple((jnp.full((rows, 1), NEG_INF, F32), jnp.zeros((rows, 1), F32),
                  jnp.zeros((rows, HEAD_PAD), F32)) for _ in chains)
    carry = lax.fori_loop(0, qi, functools.partial(step, diagonal=False), init)
    carry = step(qi, carry, diagonal=True)
    for c, (h, r) in enumerate(chains):
        _, l, acc = carry[c]
        o_ref[pl.ds(r * rows, rows), cols(h)] = (acc / l).astype(o_ref.dtype)


def mla_attention(q, k, v, batch, seq, bq, n_split=2, heads=1):
    nq = seq // bq
    width = heads * HEAD_PAD
    qspec = pl.BlockSpec((bq, width), lambda b, h, i: (b * nq + i, h))
    kvspec = pl.BlockSpec((seq, width), lambda b, h, i: (b, h))
    return pl.pallas_call(
        functools.partial(_mla_attn_kernel, bq=bq, n_split=n_split, heads=heads),
        grid=(batch, MLA_HEADS // heads, nq),
        in_specs=[qspec, kvspec, kvspec],
        out_specs=qspec,
        out_shape=jax.ShapeDtypeStruct(q.shape, BF16),
        compiler_params=_params("parallel", "parallel", "arbitrary"),
        name="mla_attention",
    )(q, k, v)


def _swa_kernel(q_ref, kc_ref, kp_ref, vc_ref, vp_ref, bias_ref, sink_ref, o_ref, *, n_win):
    w = SWA_WINDOW
    grp = SWA_HEADS // SWA_KV_HEADS
    first = pl.program_id(1) == 0
    prev_half = lax.broadcasted_iota(jnp.int32, (grp * w, 2 * w), 1) < w
    for r in range(n_win):
        rows = slice(r * w, (r + 1) * w)
        for g in range(SWA_KV_HEADS):
            kcols = slice(g * HEAD_PAD, (g + 1) * HEAD_PAD)
            if r == 0:
                kb = jnp.concatenate([kp_ref[:, kcols], kc_ref[0:w, kcols]], axis=0)
                vb = jnp.concatenate([vp_ref[:, kcols], vc_ref[0:w, kcols]], axis=0)
            else:
                kb = kc_ref[(r - 1) * w:(r + 1) * w, kcols]
                vb = vc_ref[(r - 1) * w:(r + 1) * w, kcols]
            q = jnp.concatenate(
                [q_ref[rows, (g * grp + j) * HEAD_PAD:(g * grp + j + 1) * HEAD_PAD] for j in range(grp)], axis=0)
            s = lax.dot_general(q, kb, (((1,), (1,)), ((), ())), preferred_element_type=F32)
            s = s * (SWA_DIM ** -0.5) + bias_ref[g]
            if r == 0:
                s = jnp.where(jnp.logical_and(first, prev_half), NEG_INF, s)
            sink = sink_ref[g]
            m = jnp.maximum(jnp.max(s, axis=-1, keepdims=True), sink)
            e = jnp.exp(s - m)
            p = e / (jnp.sum(e, axis=-1, keepdims=True) + jnp.exp(sink - m))
            out = jnp.dot(p.astype(BF16), vb, preferred_element_type=F32)
            for j in range(grp):
                o_ref[rows, (g * grp + j) * HEAD_PAD:(g * grp + j + 1) * HEAD_PAD] = (
                    out[j * w:(j + 1) * w].astype(o_ref.dtype))


def swa(z, batch, seq, bias, sinks, n_win):
    w = SWA_WINDOW
    rows = n_win * w
    ns = seq // rows
    qw = SWA_HEADS * HEAD_PAD
    kw = SWA_KV_HEADS * HEAD_PAD
    grp = SWA_HEADS // SWA_KV_HEADS
    cur = lambda col: pl.BlockSpec((rows, kw), lambda b, n, col=col: (b * ns + n, col // kw))
    prev = lambda col: pl.BlockSpec(
        (w, kw), lambda b, n, col=col: ((b * ns + n) * n_win - jnp.minimum(n, 1), col // kw))
    return pl.pallas_call(
        functools.partial(_swa_kernel, n_win=n_win),
        grid=(batch, ns),
        in_specs=[pl.BlockSpec((rows, qw), lambda b, n: (b * ns + n, Z_SQ // qw)),
                  cur(Z_SK), prev(Z_SK), cur(Z_SV), prev(Z_SV),
                  pl.BlockSpec((SWA_KV_HEADS, grp * w, 2 * w), lambda b, n: (0, 0, 0)),
                  pl.BlockSpec((SWA_KV_HEADS, grp * w, 1), lambda b, n: (0, 0, 0))],
        out_specs=pl.BlockSpec((rows, qw), lambda b, n: (b * ns + n, 0)),
        out_shape=jax.ShapeDtypeStruct((batch * seq, qw), BF16),
        compiler_params=_params("parallel", "arbitrary"),
        name="swa",
    )(z, z, z, z, z, bias, sinks)


def _merge_kernel(x_ref, yr_ref, ym_ref, ys_ref, gl_ref, wr_ref, wm_ref, ws_ref, wo_ref, o_ref):
    d = x_ref.shape[1]
    merged = None
    for i, (y_ref, w_ref) in enumerate(((yr_ref, wr_ref), (ym_ref, wm_ref), (ys_ref, ws_ref))):
        gate = jax.nn.sigmoid(gl_ref[:, i * d:(i + 1) * d].astype(F32))
        term = gate * jnp.dot(y_ref[...], w_ref[...], preferred_element_type=F32)
        merged = term if merged is None else merged + term
    o_ref[...] = x_ref[...] + jnp.dot(merged.astype(BF16), wo_ref[...], preferred_element_type=F32)


def merge(x, y_ret, y_mla, y_swa, z, wr, wm, ws, wo, bm):
    t, d = x.shape
    row = lambda width: pl.BlockSpec((bm, width), lambda i: (i, 0))
    full = lambda a: pl.BlockSpec(a.shape, lambda i: (0, 0))
    return pl.pallas_call(
        _merge_kernel,
        grid=(t // bm,),
        in_specs=[row(d), row(y_ret.shape[1]), row(y_mla.shape[1]), row(y_swa.shape[1]),
                  pl.BlockSpec((bm, N_BRANCH * d), lambda i: (i, Z_GATE // (N_BRANCH * d))),
                  full(wr), full(wm), full(ws), full(wo)],
        out_specs=row(d),
        out_shape=jax.ShapeDtypeStruct((t, d), F32),
        compiler_params=_params("parallel"),
        name="merge",
    )(x, y_ret, y_mla, y_swa, z, wr, wm, ws, wo)


def _xattn_kernel(x_ref, g_ref, wq_ref, kv_ref, wo_ref, o_ref):
    x = x_ref[...]
    hn = _rms(x, g_ref[...]).astype(BF16)
    q = (jnp.dot(hn, wq_ref[...], preferred_element_type=F32) * (XA_DIM ** -0.5)).astype(BF16)
    width = XA_HEADS * XA_DIM
    outs = []
    for h in range(XA_HEADS):
        cols = slice(h * XA_DIM, (h + 1) * XA_DIM)
        k = kv_ref[0, :, cols]
        v = kv_ref[0, :, width + h * XA_DIM:width + (h + 1) * XA_DIM]
        s = lax.dot_general(q[:, cols], k, (((1,), (1,)), ((), ())), preferred_element_type=F32)
        e = jnp.exp(s - jnp.max(s, axis=-1, keepdims=True))
        p = e / jnp.sum(e, axis=-1, keepdims=True)
        outs.append(jnp.dot(p.astype(BF16), v, preferred_element_type=F32).astype(BF16))
    o = jnp.concatenate(outs, axis=1)
    o_ref[...] = x + jnp.dot(o, wo_ref[...], preferred_element_type=F32)


def xattn(x, g, wq, memkv, wo, seq, bm):
    t, d = x.shape
    nsb = seq // bm
    full = lambda a: pl.BlockSpec(a.shape, lambda i: (0, 0))
    return pl.pallas_call(
        _xattn_kernel,
        grid=(t // bm,),
        in_specs=[pl.BlockSpec((bm, d), lambda i: (i, 0)),
                  pl.BlockSpec((1, d), lambda i: (0, 0)),
                  full(wq),
                  pl.BlockSpec((1,) + memkv.shape[1:], lambda i: (i // nsb, 0, 0)),
                  full(wo)],
        out_specs=pl.BlockSpec((bm, d), lambda i: (i, 0)),
        out_shape=jax.ShapeDtypeStruct((t, d), F32),
        compiler_params=_params("parallel"),
        name="xattn",
    )(x, g.reshape(1, d), wq, memkv, wo)


_PEER_CAND = [(a, b) for a in range(PEER_TOPK) for b in range(PEER_TOPK) if (a + 1) * (b + 1) <= PEER_TOPK]
_PEER_NCAND = len(_PEER_CAND)
_PEER_CAND_ROWS = -(-_PEER_NCAND // 8) * 8


def _top16_rows(s, vals_ref, idx_ref, payload=None):
    n = s.shape[0]
    rows = lax.broadcasted_iota(jnp.int32, s.shape, 0).astype(F32)
    for r in range(PEER_TOPK):
        m = jnp.max(s, axis=0, keepdims=True)
        win = jnp.min(jnp.where(s == m, rows, float(n)), axis=0, keepdims=True)
        hit = rows == win
        vals_ref[pl.ds(r, 1), :] = m
        if payload is None:
            idx_ref[pl.ds(r, 1), :] = win
        else:
            idx_ref[pl.ds(r, 1), :] = jnp.sum(jnp.where(hit, payload, 0.0), axis=0, keepdims=True)
        s = jnp.where(hit, -jnp.inf, s)


def _peer_route_kernel(x_ref, g_ref, wq_ref, keys_ref, hn_ref, gate_ref, i1_ref, i2_ref,
                       v1_ref, n1_ref, v2_ref, n2_ref, cs_ref, ci_ref, bs_ref, bi_ref,
                       gt_ref, i1t_ref, i2t_ref):
    bm = x_ref.shape[0]
    hn = _rms(x_ref[...], g_ref[...]).astype(BF16)
    hn_ref[...] = hn
    q = jnp.dot(hn, wq_ref[...], preferred_element_type=F32).astype(BF16)
    k = PEER_TOPK
    cs_ref[...] = jnp.full(cs_ref.shape, -jnp.inf, F32)
    ci_ref[...] = jnp.zeros(ci_ref.shape, F32)
    for h in range(PEER_HEADS):
        qh = q[:, h * PEER_DKEY:(h + 1) * PEER_DKEY]
        for p, (v_ref, n_ref) in enumerate(((v1_ref, n1_ref), (v2_ref, n2_ref))):
            s = lax.dot_general(keys_ref[2 * h + p], qh, (((1,), (1,)), ((), ())),
                                preferred_element_type=F32)
            _top16_rows(s, v_ref, n_ref)
        for c, (a, b) in enumerate(_PEER_CAND):
            cs_ref[pl.ds(c, 1), :] = v1_ref[pl.ds(a, 1), :] + v2_ref[pl.ds(b, 1), :]
            ci_ref[pl.ds(c, 1), :] = n1_ref[pl.ds(a, 1), :] * PEER_KEYS + n2_ref[pl.ds(b, 1), :]
        _top16_rows(cs_ref[...], bs_ref, bi_ref, payload=ci_ref[...])
        best = bs_ref[...]
        e = jnp.exp(best - best[0:1, :])
        gt_ref[pl.ds(h * k, k), :] = e / jnp.sum(e, axis=0, keepdims=True)
        flat = bi_ref[...]
        first = jnp.floor(flat * (1.0 / PEER_KEYS))
        i1t_ref[pl.ds(h * k, k), :] = first
        i2t_ref[pl.ds(h * k, k), :] = flat - first * PEER_KEYS
    gate_ref[...] = gt_ref[...].T
    i1_ref[...] = i1t_ref[...].T
    i2_ref[...] = i2t_ref[...].T


def peer_route(x, g, wq, keys, bm):
    t, d = x.shape
    slots = PEER_HEADS * PEER_TOPK
    row = lambda width: pl.BlockSpec((bm, width), lambda i: (i, 0))
    tk = lambda: pltpu.VMEM((PEER_TOPK, bm), F32)
    tall = lambda rows: pltpu.VMEM((rows, bm), F32)
    return pl.pallas_call(
        _peer_route_kernel,
        grid=(t // bm,),
        in_specs=[row(d), pl.BlockSpec((1, d), lambda i: (0, 0)),
                  pl.BlockSpec(wq.shape, lambda i: (0, 0)),
                  pl.BlockSpec(keys.shape, lambda i: (0, 0, 0))],
        out_specs=[row(d), row(slots), row(slots), row(slots)],
        out_shape=[jax.ShapeDtypeStruct((t, d), BF16)] + [jax.ShapeDtypeStruct((t, slots), F32)] * 3,
        scratch_shapes=[tk(), tk(), tk(), tk(), tall(_PEER_CAND_ROWS), tall(_PEER_CAND_ROWS),
                        tk(), tk(), tall(slots), tall(slots), tall(slots)],
        compiler_params=_params("parallel"),
        name="peer_route",
    )(x, g.reshape(1, d), wq, keys)


G_PITCH = 136
G_TOKENS_PER_MATMUL = 16


def _peer_gates_kernel(gate_ref, i1_ref, i2_ref, o_ref, s_ref):
    bm = gate_ref.shape[0]
    nk = PEER_KEYS
    slots = gate_ref.shape[2]
    sub = G_TOKENS_PER_MATMUL
    key = lax.broadcasted_iota(jnp.int32, (sub, nk, slots), 1).astype(F32)

    def build(c, carry):
        tok = pl.ds(pl.multiple_of(c * sub, sub), sub)
        wa = jnp.where(i1_ref[tok] == key, gate_ref[tok], 0.0).astype(BF16)
        wb = jnp.where(i2_ref[tok] == key, 1.0, 0.0).astype(BF16)
        g = lax.dot_general(wa, wb, (((2,), (2,)), ((0,), (0,))),
                            preferred_element_type=F32)
        for u in range(sub):
            s_ref[pl.ds(pl.multiple_of((c * sub + u) * G_PITCH, 8), nk), :] = g[u]
        return carry

    lax.fori_loop(0, bm // sub, build, 0, unroll=2)
    for i in range(nk):
        o_ref[:, i * nk:(i + 1) * nk] = s_ref[pl.ds(i, bm, stride=G_PITCH), :].astype(o_ref.dtype)


def peer_gates(gate, i1, i2, bm):
    t, slots = gate.shape
    n_exp = PEER_KEYS * PEER_KEYS
    spec = pl.BlockSpec((bm, 1, slots), lambda i: (i, 0, 0))
    r3 = lambda a: a.reshape(t, 1, slots)
    return pl.pallas_call(
        _peer_gates_kernel,
        grid=(t // bm,),
        in_specs=[spec, spec, spec],
        out_specs=pl.BlockSpec((bm, n_exp), lambda i: (i, 0)),
        out_shape=jax.ShapeDtypeStruct((t, n_exp), BF16),
        scratch_shapes=[pltpu.VMEM((bm * G_PITCH, PEER_KEYS), F32)],
        compiler_params=_params("parallel"),
        name="peer_gates",
    )(r3(gate), r3(i1), r3(i2))


def _peer_expert_kernel(x_ref, hn_ref, gm_ref, ut_ref, v_ref, o_ref, acc_ref):
    j = pl.program_id(1)

    @pl.when(j == 0)
    def _():
        acc_ref[...] = jnp.zeros_like(acc_ref)

    a = jnp.dot(hn_ref[...], ut_ref[...], preferred_element_type=F32)
    act = 0.5 * a * (1.0 + lax.erf(a * (2.0 ** -0.5)))
    p = (act * gm_ref[...].astype(F32)).astype(BF16)
    acc_ref[...] += jnp.dot(p, v_ref[...], preferred_element_type=F32)

    @pl.when(j == pl.num_programs(1) - 1)
    def _():
        o_ref[...] = x_ref[...] + acc_ref[...]


def peer_experts(x, hn, gm, ut, v, bm, bn):
    t, d = x.shape
    n_exp = ut.shape[1]
    return pl.pallas_call(
        _peer_expert_kernel,
        grid=(t // bm, n_exp // bn),
        in_specs=[pl.BlockSpec((bm, d), lambda i, j: (i, 0)),
                  pl.BlockSpec((bm, d), lambda i, j: (i, 0)),
                  pl.BlockSpec((bm, bn), lambda i, j: (i, j)),
                  pl.BlockSpec((d, bn), lambda i, j: (0, j)),
                  pl.BlockSpec((bn, d), lambda i, j: (j, 0))],
        out_specs=pl.BlockSpec((bm, d), lambda i, j: (i, 0)),
        out_shape=jax.ShapeDtypeStruct((t, d), F32),
        scratch_shapes=[pltpu.VMEM((bm, d), F32)],
        compiler_params=_params("parallel", "arbitrary"),
        name="peer_experts",
    )(x, hn, gm, ut, v)


def _final_norm_kernel(x_ref, g_ref, o_ref):
    o_ref[...] = _rms(x_ref[...], g_ref[...])


def final_norm(x, g, bm):
    t, d = x.shape
    return pl.pallas_call(
        _final_norm_kernel,
        grid=(t // bm,),
        in_specs=[pl.BlockSpec((bm, d), lambda i: (i, 0)), pl.BlockSpec((1, d), lambda i: (0, 0))],
        out_specs=pl.BlockSpec((bm, d), lambda i: (i, 0)),
        out_shape=jax.ShapeDtypeStruct((t, d), F32),
        compiler_params=_params("parallel"),
        name="final_norm",
    )(x, g.reshape(1, d))


def _pad_heads(w, heads, dim, axis):
    shape = w.shape[:axis] + (heads, dim) + w.shape[axis + 1:]
    w = w.reshape(shape)
    pad = [(0, 0)] * w.ndim
    pad[axis + 1] = (0, HEAD_PAD - dim)
    w = jnp.pad(w, pad)
    return w.reshape(w.shape[:axis] + (heads * HEAD_PAD,) + w.shape[axis + 2:])


def _rot_half_cols(w):
    half = w.shape[-1] // 2
    return jnp.concatenate([-w[..., half:], w[..., :half]], axis=-1)


def _pack_w_in(w_in):
    rq, rk, rv, rg, qa, kva, kr, sq, sk, sv, gate = jnp.split(
        w_in, [int(p) for p in np.cumsum(
            [512, 512, 512, 512, MLA_Q_RANK, MLA_KV_RANK, MLA_ROPE,
             SWA_HEADS * SWA_DIM, SWA_KV_HEADS * SWA_DIM, SWA_KV_HEADS * SWA_DIM])], axis=-1)
    place = lambda w: jnp.pad(w, ((0, 0), (0, 0), (MLA_NOPE, HEAD_PAD - MLA_NOPE - MLA_ROPE)))
    cols = [rq, rk, rv, rg, _pad_heads(sq, SWA_HEADS, SWA_DIM, 2), gate, qa,
            _pad_heads(sk, SWA_KV_HEADS, SWA_DIM, 2), _pad_heads(sv, SWA_KV_HEADS, SWA_DIM, 2),
            kva, place(kr), place(_rot_half_cols(kr))]
    out = jnp.concatenate(cols, axis=-1).astype(BF16)
    assert out.shape[-1] == Z_COLS
    return out


def _rope_tables(seq, half):
    inv_freq = jnp.power(ROPE_BASE, -jnp.arange(half, dtype=F32) / half)
    ang = jnp.arange(seq, dtype=F32)[:, None] * inv_freq[None, :]
    return jnp.cos(ang), jnp.sin(ang)


def _t5_buckets(rel):
    n = np.maximum(rel, 0)
    max_exact = REL_BUCKETS // 2
    large = max_exact + (np.log(np.maximum(n, 1) / max_exact) / np.log(REL_MAX_DIST / max_exact)
                         * (REL_BUCKETS - max_exact)).astype(np.int32)
    large = np.minimum(large, REL_BUCKETS - 1)
    return np.where(n < max_exact, n, large).astype(np.int32)


def _swa_bias(rel_bias):
    w = SWA_WINDOW
    rel = (np.arange(w)[:, None] + w) - np.arange(2 * w)[None, :]
    bias = rel_bias[_t5_buckets(rel)].astype(F32)
    bias = jnp.where(((rel >= 0) & (rel < w))[:, :, None], bias, NEG_INF)
    bias = bias.transpose(2, 0, 1)
    return bias.reshape(SWA_KV_HEADS, (SWA_HEADS // SWA_KV_HEADS) * w, 2 * w)


def _retention_tables():
    c = RET_CHUNK
    log_gamma = np.log(1.0 - np.exp2(-5.0 - np.arange(RET_HEADS, dtype=np.float64)))
    idx = np.arange(c, dtype=np.float64)
    diff = idx[:, None] - idx[None, :]
    intra = np.where(diff >= 0, np.exp(log_gamma[:, None, None] * np.maximum(diff, 0.0)), 0.0)
    qw = np.exp(log_gamma[:, None] * (idx + 1.0)[None, :])
    kw = np.exp(log_gamma[:, None] * (c - 1.0 - idx)[None, :])
    bc = lambda a: jnp.asarray(np.broadcast_to(a[:, :, None], (RET_HEADS, c, c)), F32)
    chunk_decay = tuple(float(v) for v in np.exp(log_gamma * c))
    return jnp.asarray(intra, F32), bc(qw), bc(kw), chunk_decay


def kernel(x, mem, rel_bias, g_mix, w_in, mla_q_norm, w_mla_qb, mla_kv_norm, w_mla_kvb, swa_sinks,
           w_branch_ret, w_branch_mla, w_branch_swa, w_out, g_xattn, g_mem, w_xq, w_xkv, w_xo,
           g_ffn, w_peer_query, peer_sub_keys, peer_u, peer_v, g_final):
    batch, seq, d = x.shape
    depth = w_in.shape[0]
    t = batch * seq
    mem_len = mem.shape[1]
    bm = min(512, seq)
    big = min(1024, seq)

    w_in_p = _pack_w_in(w_in)
    qb = w_mla_qb.reshape(depth, MLA_Q_RANK, MLA_HEADS, MLA_NOPE + MLA_ROPE)
    q_nope, q_rope = qb[..., :MLA_NOPE], qb[..., MLA_NOPE:]
    zpad = jnp.zeros(qb.shape[:3] + (HEAD_PAD - MLA_NOPE - MLA_ROPE,), F32)
    flat = lambda w: w.reshape(w.shape[:2] + (MLA_HEADS * HEAD_PAD,)).astype(BF16)
    w_q1 = flat(jnp.concatenate([q_nope, q_rope, zpad], axis=-1))
    w_q2 = flat(jnp.concatenate([jnp.zeros_like(q_nope), _rot_half_cols(q_rope), zpad], axis=-1))
    kvb = w_mla_kvb.reshape(depth, MLA_KV_RANK, MLA_HEADS, MLA_NOPE + MLA_V)
    w_k = flat(jnp.pad(kvb[..., :MLA_NOPE], ((0, 0),) * 3 + ((0, HEAD_PAD - MLA_NOPE),)))
    w_v = flat(jnp.pad(kvb[..., MLA_NOPE:], ((0, 0),) * 3 + ((0, HEAD_PAD - MLA_V),)))
    w_bret = w_branch_ret.astype(BF16)
    w_bmla = _pad_heads(w_branch_mla, MLA_HEADS, MLA_V, 1).astype(BF16)
    w_bswa = _pad_heads(w_branch_swa, SWA_HEADS, SWA_DIM, 1).astype(BF16)
    w_out_b = w_out.astype(BF16)
    w_xq_b, w_xkv_b, w_xo_b = w_xq.astype(BF16), w_xkv.astype(BF16), w_xo.astype(BF16)
    w_pq = w_peer_query.astype(BF16)
    half = PEER_DKEY // 2
    keys = jnp.stack([jnp.pad(peer_sub_keys[:, :, 0], ((0, 0),) * 3 + ((0, half),)),
                      jnp.pad(peer_sub_keys[:, :, 1], ((0, 0),) * 3 + ((half, 0),))], axis=2)
    keys = keys.reshape(depth, 2 * PEER_HEADS, PEER_KEYS, PEER_DKEY).astype(BF16)
    peer_ut = jnp.swapaxes(peer_u, 1, 2).astype(BF16)
    peer_vb = peer_v.astype(BF16)

    cos64, sin64 = _rope_tables(seq, RET_DIM // 2)
    ret_cos = jnp.concatenate([cos64, cos64], axis=1)
    ret_sin = jnp.concatenate([-sin64, sin64], axis=1)
    cos16, sin16 = _rope_tables(seq, MLA_ROPE // 2)
    tail = jnp.zeros((seq, HEAD_PAD - MLA_NOPE - MLA_ROPE), F32)
    mla_cos = jnp.concatenate([jnp.ones((seq, MLA_NOPE), F32), cos16, cos16, tail], axis=1)
    mla_sin = jnp.concatenate([jnp.zeros((seq, MLA_NOPE), F32), sin16, sin16, tail], axis=1)
    intra, ret_qw, ret_kw, chunk_decay = _retention_tables()
    swa_bias = _swa_bias(rel_bias)
    grp = SWA_HEADS // SWA_KV_HEADS

    x = x.reshape(t, d)
    mem2 = mem.reshape(batch * mem_len, d)
    for l in range(depth):
        z = norm_matmul(x, g_mix[l], w_in_p[l], bm=big, bn=Z_COLS // 3, out_dtype=BF16)
        y_ret = retention(z, batch, seq, ret_cos, ret_sin, intra, ret_qw, ret_kw, chunk_decay)
        q = mla_q(z, mla_q_norm[l], w_q1[l], w_q2[l], mla_cos, mla_sin, seq, bm)
        k, v = mla_kv(z, mla_kv_norm[l], w_k[l], w_v[l], mla_cos, mla_sin, seq, bm)
        y_mla = mla_attention(q, k, v, batch, seq, bq=big)
        sinks = jnp.broadcast_to(swa_sinks[l].reshape(SWA_KV_HEADS, grp, 1, 1),
                                 (SWA_KV_HEADS, grp, SWA_WINDOW, 1)).reshape(SWA_KV_HEADS, grp * SWA_WINDOW, 1)
        y_swa = swa(z, batch, seq, swa_bias, sinks, n_win=min(4, seq // SWA_WINDOW))
        x = merge(x, y_ret, y_mla, y_swa, z, w_bret[l], w_bmla[l], w_bswa[l], w_out_b[l], bm)

        memkv = norm_matmul(mem2, g_mem[l], w_xkv_b[l], bm=min(512, batch * mem_len),
                            bn=w_xkv_b.shape[2], out_dtype=BF16).reshape(batch, mem_len, -1)
        x = xattn(x, g_xattn[l], w_xq_b[l], memkv, w_xo_b[l], seq, bm)

        hn, gate, i1, i2 = peer_route(x, g_ffn[l], w_pq[l], keys[l], bm=min(256, seq))
        gm = peer_gates(gate, i1, i2, bm=min(128, seq))
        x = peer_experts(x, hn, gm, peer_ut[l], peer_vb[l], bm=big, bn=1024)
    return final_norm(x, g_final, bm).reshape(batch, seq, d)
```

```python
import functools
import math

import numpy as np
import jax
import jax.numpy as jnp
from jax import lax
from jax.experimental import pallas as pl
from jax.experimental.pallas import tpu as pltpu

F32 = jnp.float32
BF16 = jnp.bfloat16

EPS = 1e-6
NEG_INF = -1e30
ROPE_BASE = 10000.0
N_BRANCH = 3

RET_HEADS = 4
RET_DIM = 128
RET_CHUNK = 128

MLA_HEADS = 8
MLA_NOPE = 64
MLA_ROPE = 32
MLA_V = 64
MLA_Q_RANK = 256
MLA_KV_RANK = 128

SWA_HEADS = 8
SWA_KV_HEADS = 2
SWA_DIM = 64
SWA_WINDOW = 128
REL_BUCKETS = 32
REL_MAX_DIST = 128

XA_HEADS = 4
XA_DIM = 128

PEER_HEADS = 8
PEER_KEYS = 128
PEER_DKEY = 128
PEER_TOPK = 16

LANES = 128
HEAD_PAD = LANES

Z_RQ, Z_RK, Z_RV, Z_RG = 0, 512, 1024, 1536
Z_SQ = 2048
Z_GATE = 3072
Z_QA = 6144
Z_SK = 6400
Z_SV = 6656
Z_KVA = 6912
Z_KRP = 7040
Z_KRR = 7168
Z_COLS = 7296

VMEM_LIMIT = 56 * 1024 * 1024


def _params(*sem):
    return pltpu.CompilerParams(dimension_semantics=sem, vmem_limit_bytes=VMEM_LIMIT)


def _rms(x, g=None):
    y = x * lax.rsqrt(jnp.mean(x * x, axis=-1, keepdims=True) + EPS)
    return y if g is None else y * g


def _norm_matmul_kernel(x_ref, g_ref, w_ref, o_ref, h_ref):
    @pl.when(pl.program_id(1) == 0)
    def _():
        h_ref[...] = _rms(x_ref[...].astype(F32), g_ref[...]).astype(BF16)

    o_ref[...] = jnp.dot(h_ref[...], w_ref[...], preferred_element_type=F32).astype(o_ref.dtype)


def norm_matmul(x, g, w, bm, bn, out_dtype):
    m, k = x.shape
    n = w.shape[1]
    return pl.pallas_call(
        _norm_matmul_kernel,
        grid=(m // bm, n // bn),
        in_specs=[pl.BlockSpec((bm, k), lambda i, j: (i, 0)),
                  pl.BlockSpec((1, k), lambda i, j: (0, 0)),
                  pl.BlockSpec((k, bn), lambda i, j: (0, j))],
        out_specs=pl.BlockSpec((bm, bn), lambda i, j: (i, j)),
        out_shape=jax.ShapeDtypeStruct((m, n), out_dtype),
        scratch_shapes=[pltpu.VMEM((bm, k), BF16)],
        compiler_params=_params("parallel", "arbitrary"),
        name="norm_matmul",
    )(x, g.reshape(1, k), w)


def _retention_kernel(q_ref, k_ref, v_ref, g_ref, cos_ref, sin_ref, intra_ref, qw_ref, kw_ref,
                      o_ref, state_ref, *, chunk_decay):
    @pl.when(pl.program_id(1) == 0)
    def _():
        state_ref[...] = jnp.zeros_like(state_ref)

    cos = cos_ref[...]
    sin = sin_ref[...]
    half = RET_DIM // 2

    def rope(t):
        return t * cos + pltpu.roll(t, half, 1) * sin

    for h in range(RET_HEADS):
        cols = slice(h * RET_DIM, (h + 1) * RET_DIM)
        q = rope(q_ref[:, cols].astype(F32)).astype(BF16)
        k = rope(k_ref[:, cols].astype(F32)) * (RET_DIM ** -0.5)
        v = v_ref[:, cols]
        state = state_ref[h]
        scores = lax.dot_general(q, k.astype(BF16), (((1,), (1,)), ((), ())),
                                 preferred_element_type=F32) * intra_ref[h]
        inner = jnp.dot(scores.astype(BF16), v, preferred_element_type=F32)
        cross = jnp.dot(q, state.astype(BF16), preferred_element_type=F32) * qw_ref[h]
        kv = lax.dot_general((k * kw_ref[h]).astype(BF16), v, (((0,), (0,)), ((), ())),
                             preferred_element_type=F32)
        state_ref[h] = chunk_decay[h] * state + kv
        y = _rms(inner + cross)
        gate = g_ref[:, cols].astype(F32)
        o_ref[:, cols] = (y * (gate * jax.nn.sigmoid(gate))).astype(o_ref.dtype)


def retention(z, batch, seq, cos, sin, intra, qw, kw, chunk_decay):
    c = RET_CHUNK
    nc = seq // c
    w = RET_HEADS * RET_DIM
    zspec = lambda col: pl.BlockSpec((c, w), lambda b, n, col=col: (b * nc + n, col // w))
    tab = pl.BlockSpec((c, RET_DIM), lambda b, n: (n, 0))
    const = pl.BlockSpec((RET_HEADS, c, c), lambda b, n: (0, 0, 0))
    return pl.pallas_call(
        functools.partial(_retention_kernel, chunk_decay=chunk_decay),
        grid=(batch, nc),
        in_specs=[zspec(Z_RQ), zspec(Z_RK), zspec(Z_RV), zspec(Z_RG), tab, tab, const, const, const],
        out_specs=pl.BlockSpec((c, w), lambda b, n: (b * nc + n, 0)),
        out_shape=jax.ShapeDtypeStruct((batch * seq, w), BF16),
        scratch_shapes=[pltpu.VMEM((RET_HEADS, RET_DIM, RET_DIM), F32)],
        compiler_params=_params("parallel", "arbitrary"),
        name="retention",
    )(z, z, z, z, cos, sin, intra, qw, kw)


def _mla_q_kernel(qa_ref, g_ref, w1_ref, w2_ref, cos_ref, sin_ref, o_ref):
    qn = _rms(qa_ref[...].astype(F32), g_ref[...]).astype(BF16)
    a = jnp.dot(qn, w1_ref[...], preferred_element_type=F32)
    b = jnp.dot(qn, w2_ref[...], preferred_element_type=F32)
    cos = cos_ref[...]
    sin = sin_ref[...]
    scale = (MLA_NOPE + MLA_ROPE) ** -0.5 * math.log2(math.e)
    for h in range(MLA_HEADS):
        cols = slice(h * HEAD_PAD, (h + 1) * HEAD_PAD)
        o_ref[:, cols] = ((a[:, cols] * cos + b[:, cols] * sin) * scale).astype(o_ref.dtype)


def mla_q(z, g, w1, w2, cos, sin, seq, bm):
    t = z.shape[0]
    wd = MLA_HEADS * HEAD_PAD
    nsb = seq // bm
    return pl.pallas_call(
        _mla_q_kernel,
        grid=(t // bm,),
        in_specs=[pl.BlockSpec((bm, MLA_Q_RANK), lambda i: (i, Z_QA // MLA_Q_RANK)),
                  pl.BlockSpec((1, MLA_Q_RANK), lambda i: (0, 0)),
                  pl.BlockSpec((MLA_Q_RANK, wd), lambda i: (0, 0)),
                  pl.BlockSpec((MLA_Q_RANK, wd), lambda i: (0, 0)),
                  pl.BlockSpec((bm, HEAD_PAD), lambda i: (i % nsb, 0)),
                  pl.BlockSpec((bm, HEAD_PAD), lambda i: (i % nsb, 0))],
        out_specs=pl.BlockSpec((bm, wd), lambda i: (i, 0)),
        out_shape=jax.ShapeDtypeStruct((t, wd), BF16),
        compiler_params=_params("parallel"),
        name="mla_q",
    )(z, g.reshape(1, -1), w1, w2, cos, sin)


def _mla_kv_kernel(kva_ref, krp_ref, krr_ref, g_ref, wk_ref, wv_ref, cos_ref, sin_ref, k_ref, v_ref):
    kvn = _rms(kva_ref[...].astype(F32), g_ref[...]).astype(BF16)
    k = jnp.dot(kvn, wk_ref[...], preferred_element_type=F32)
    krope = krp_ref[...].astype(F32) * cos_ref[...] + krr_ref[...].astype(F32) * sin_ref[...]
    for h in range(MLA_HEADS):
        cols = slice(h * HEAD_PAD, (h + 1) * HEAD_PAD)
        k_ref[:, cols] = (k[:, cols] + krope).astype(k_ref.dtype)
    v_ref[...] = jnp.dot(kvn, wv_ref[...], preferred_element_type=F32).astype(v_ref.dtype)


def mla_kv(z, g, wk, wv, cos, sin, seq, bm):
    t = z.shape[0]
    wd = MLA_HEADS * HEAD_PAD
    nsb = seq // bm
    zs = lambda col: pl.BlockSpec((bm, LANES), lambda i, col=col: (i, col // LANES))
    tab = pl.BlockSpec((bm, HEAD_PAD), lambda i: (i % nsb, 0))
    wspec = pl.BlockSpec((MLA_KV_RANK, wd), lambda i: (0, 0))
    out = pl.BlockSpec((bm, wd), lambda i: (i, 0))
    return pl.pallas_call(
        _mla_kv_kernel,
        grid=(t // bm,),
        in_specs=[zs(Z_KVA), zs(Z_KRP), zs(Z_KRR), pl.BlockSpec((1, MLA_KV_RANK), lambda i: (0, 0)),
                  wspec, wspec, tab, tab],
        out_specs=[out, out],
        out_shape=[jax.ShapeDtypeStruct((t, wd), BF16)] * 2,
        compiler_params=_params("parallel"),
        name="mla_kv",
    )(z, z, z, g.reshape(1, -1), wk, wv, cos, sin)


def _mla_attn_kernel(q_ref, k_ref, v_ref, o_ref, *, bq, n_split, heads):
    qi = pl.program_id(2)
    rows = bq // n_split
    chains = [(h, r) for h in range(heads) for r in range(n_split)]
    cols = lambda h: slice(h * HEAD_PAD, (h + 1) * HEAD_PAD)
    qs = [q_ref[pl.ds(r * rows, rows), cols(h)] for h, r in chains]

    def step(j, carry, diagonal):
        start = pl.multiple_of(j * bq, bq)
        out = []
        for c, (h, r) in enumerate(chains):
            m, l, acc = carry[c]
            n_keys = (r + 1) * rows if diagonal else bq
            kb = k_ref[pl.ds(start, n_keys), cols(h)]
            vb = v_ref[pl.ds(start, n_keys), cols(h)]
            s = lax.dot_general(qs[c], kb, (((1,), (1,)), ((), ())), preferred_element_type=F32)
            if diagonal:
                row = r * rows + lax.broadcasted_iota(jnp.int32, (rows, n_keys), 0)
                col = lax.broadcasted_iota(jnp.int32, (rows, n_keys), 1)
                s = jnp.where(col <= row, s, NEG_INF)
            m_new = jnp.maximum(m, jnp.max(s, axis=-1, keepdims=True))
            alpha = jnp.exp2(m - m_new)
            p = jnp.exp2(s - m_new)
            l = alpha * l + jnp.sum(p, axis=-1, keepdims=True)
            acc = alpha * acc + jnp.dot(p.astype(BF16), vb, preferred_element_type=F32)
            out.append((m_new, l, acc))
        return tuple(out)

    init = tuple((jnp.full((rows, 1), NEG_INF, F32), jnp.zeros((rows, 1), F32),
                  jnp.zeros((rows, HEAD_PAD), F32)) for _ in chains)
    carry = lax.fori_loop(0, qi, functools.partial(step, diagonal=False), init)
    carry = step(qi, carry, diagonal=True)
    for c, (h, r) in enumerate(chains):
        _, l, acc = carry[c]
        o_ref[pl.ds(r * rows, rows), cols(h)] = (acc / l).astype(o_ref.dtype)


def mla_attention(q, k, v, batch, seq, bq, n_split=2, heads=1):
    nq = seq // bq
    width = heads * HEAD_PAD
    qspec = pl.BlockSpec((bq, width), lambda b, h, i: (b * nq + i, h))
    kvspec = pl.BlockSpec((seq, width), lambda b, h, i: (b, h))
    return pl.pallas_call(
        functools.partial(_mla_attn_kernel, bq=bq, n_split=n_split, heads=heads),
        grid=(batch, MLA_HEADS // heads, nq),
        in_specs=[qspec, kvspec, kvspec],
        out_specs=qspec,
        out_shape=jax.ShapeDtypeStruct(q.shape, BF16),
        compiler_params=_params("parallel", "parallel", "arbitrary"),
        name="mla_attention",
    )(q, k, v)


def _swa_kernel(q_ref, kc_ref, kp_ref, vc_ref, vp_ref, bias_ref, sink_ref, o_ref, *, n_win):
    w = SWA_WINDOW
    grp = SWA_HEADS // SWA_KV_HEADS
    first = pl.program_id(1) == 0
    prev_half = lax.broadcasted_iota(jnp.int32, (grp * w, 2 * w), 1) < w
    for r in range(n_win):
        rows = slice(r * w, (r + 1) * w)
        for g in range(SWA_KV_HEADS):
            kcols = slice(g * HEAD_PAD, (g + 1) * HEAD_PAD)
            if r == 0:
                kb = jnp.concatenate([kp_ref[:, kcols], kc_ref[0:w, kcols]], axis=0)
                vb = jnp.concatenate([vp_ref[:, kcols], vc_ref[0:w, kcols]], axis=0)
            else:
                kb = kc_ref[(r - 1) * w:(r + 1) * w, kcols]
                vb = vc_ref[(r - 1) * w:(r + 1) * w, kcols]
            q = jnp.concatenate(
                [q_ref[rows, (g * grp + j) * HEAD_PAD:(g * grp + j + 1) * HEAD_PAD] for j in range(grp)], axis=0)
            s = lax.dot_general(q, kb, (((1,), (1,)), ((), ())), preferred_element_type=F32)
            s = s * (SWA_DIM ** -0.5) + bias_ref[g]
            if r == 0:
                s = jnp.where(jnp.logical_and(first, prev_half), NEG_INF, s)
            sink = sink_ref[g]
            m = jnp.maximum(jnp.max(s, axis=-1, keepdims=True), sink)
            e = jnp.exp(s - m)
            p = e / (jnp.sum(e, axis=-1, keepdims=True) + jnp.exp(sink - m))
            out = jnp.dot(p.astype(BF16), vb, preferred_element_type=F32)
            for j in range(grp):
                o_ref[rows, (g * grp + j) * HEAD_PAD:(g * grp + j + 1) * HEAD_PAD] = (
                    out[j * w:(j + 1) * w].astype(o_ref.dtype))


def swa(z, batch, seq, bias, sinks, n_win):
    w = SWA_WINDOW
    rows = n_win * w
    ns = seq // rows
    qw = SWA_HEADS * HEAD_PAD
    kw = SWA_KV_HEADS * HEAD_PAD
    grp = SWA_HEADS // SWA_KV_HEADS
    cur = lambda col: pl.BlockSpec((rows, kw), lambda b, n, col=col: (b * ns + n, col // kw))
    prev = lambda col: pl.BlockSpec(
        (w, kw), lambda b, n, col=col: ((b * ns + n) * n_win - jnp.minimum(n, 1), col // kw))
    return pl.pallas_call(
        functools.partial(_swa_kernel, n_win=n_win),
        grid=(batch, ns),
        in_specs=[pl.BlockSpec((rows, qw), lambda b, n: (b * ns + n, Z_SQ // qw)),
                  cur(Z_SK), prev(Z_SK), cur(Z_SV), prev(Z_SV),
                  pl.BlockSpec((SWA_KV_HEADS, grp * w, 2 * w), lambda b, n: (0, 0, 0)),
                  pl.BlockSpec((SWA_KV_HEADS, grp * w, 1), lambda b, n: (0, 0, 0))],
        out_specs=pl.BlockSpec((rows, qw), lambda b, n: (b * ns + n, 0)),
        out_shape=jax.ShapeDtypeStruct((batch * seq, qw), BF16),
        compiler_params=_params("parallel", "arbitrary"),
        name="swa",
    )(z, z, z, z, z, bias, sinks)


def _merge_kernel(x_ref, yr_ref, ym_ref, ys_ref, gl_ref, wr_ref, wm_ref, ws_ref, wo_ref, o_ref):
    d = x_ref.shape[1]
    merged = None
    for i, (y_ref, w_ref) in enumerate(((yr_ref, wr_ref), (ym_ref, wm_ref), (ys_ref, ws_ref))):
        gate = jax.nn.sigmoid(gl_ref[:, i * d:(i + 1) * d].astype(F32))
        term = gate * jnp.dot(y_ref[...], w_ref[...], preferred_element_type=F32)
        merged = term if merged is None else merged + term
    o_ref[...] = x_ref[...] + jnp.dot(merged.astype(BF16), wo_ref[...], preferred_element_type=F32)


def merge(x, y_ret, y_mla, y_swa, z, wr, wm, ws, wo, bm):
    t, d = x.shape
    row = lambda width: pl.BlockSpec((bm, width), lambda i: (i, 0))
    full = lambda a: pl.BlockSpec(a.shape, lambda i: (0, 0))
    return pl.pallas_call(
        _merge_kernel,
        grid=(t // bm,),
        in_specs=[row(d), row(y_ret.shape[1]), row(y_mla.shape[1]), row(y_swa.shape[1]),
                  pl.BlockSpec((bm, N_BRANCH * d), lambda i: (i, Z_GATE // (N_BRANCH * d))),
                  full(wr), full(wm), full(ws), full(wo)],
        out_specs=row(d),
        out_shape=jax.ShapeDtypeStruct((t, d), F32),
        compiler_params=_params("parallel"),
        name="merge",
    )(x, y_ret, y_mla, y_swa, z, wr, wm, ws, wo)


def _xattn_kernel(x_ref, g_ref, wq_ref, kv_ref, wo_ref, o_ref):
    x = x_ref[...]
    hn = _rms(x, g_ref[...]).astype(BF16)
    q = (jnp.dot(hn, wq_ref[...], preferred_element_type=F32) * (XA_DIM ** -0.5)).astype(BF16)
    width = XA_HEADS * XA_DIM
    outs = []
    for h in range(XA_HEADS):
        cols = slice(h * XA_DIM, (h + 1) * XA_DIM)
        k = kv_ref[0, :, cols]
        v = kv_ref[0, :, width + h * XA_DIM:width + (h + 1) * XA_DIM]
        s = lax.dot_general(q[:, cols], k, (((1,), (1,)), ((), ())), preferred_element_type=F32)
        e = jnp.exp(s - jnp.max(s, axis=-1, keepdims=True))
        p = e / jnp.sum(e, axis=-1, keepdims=True)
        outs.append(jnp.dot(p.astype(BF16), v, preferred_element_type=F32).astype(BF16))
    o = jnp.concatenate(outs, axis=1)
    o_ref[...] = x + jnp.dot(o, wo_ref[...], preferred_element_type=F32)


def xattn(x, g, wq, memkv, wo, seq, bm):
    t, d = x.shape
    nsb = seq // bm
    full = lambda a: pl.BlockSpec(a.shape, lambda i: (0, 0))
    return pl.pallas_call(
        _xattn_kernel,
        grid=(t // bm,),
        in_specs=[pl.BlockSpec((bm, d), lambda i: (i, 0)),
                  pl.BlockSpec((1, d), lambda i: (0, 0)),
                  full(wq),
                  pl.BlockSpec((1,) + memkv.shape[1:], lambda i: (i // nsb, 0, 0)),
                  full(wo)],
        out_specs=pl.BlockSpec((bm, d), lambda i: (i, 0)),
        out_shape=jax.ShapeDtypeStruct((t, d), F32),
        compiler_params=_params("parallel"),
        name="xattn",
    )(x, g.reshape(1, d), wq, memkv, wo)


_PEER_CAND = [(a, b) for a in range(PEER_TOPK) for b in range(PEER_TOPK) if (a + 1) * (b + 1) <= PEER_TOPK]
_PEER_NCAND = len(_PEER_CAND)
_PEER_CAND_ROWS = -(-_PEER_NCAND // 8) * 8


def _top16_rows(s, vals_ref, idx_ref, payload=None):
    n, width = s.shape
    rows = lax.broadcasted_iota(jnp.int32, (n, LANES), 0).astype(F32)
    for c in range(width // LANES):
        lanes = slice(c * LANES, (c + 1) * LANES)
        sc = s[:, lanes]
        pc = None if payload is None else payload[:, lanes]
        for r in range(PEER_TOPK):
            m = jnp.max(sc, axis=0, keepdims=True)
            win = jnp.min(jnp.where(sc == m, rows, float(n)), axis=0, keepdims=True)
            hit = rows == win
            vals_ref[pl.ds(r, 1), lanes] = m
            if pc is None:
                idx_ref[pl.ds(r, 1), lanes] = win
            else:
                idx_ref[pl.ds(r, 1), lanes] = jnp.sum(jnp.where(hit, pc, 0.0), axis=0, keepdims=True)
            sc = jnp.where(hit, -jnp.inf, sc)


def _peer_route_kernel(x_ref, g_ref, wq_ref, keys_ref, hn_ref, gate_ref, i1_ref, i2_ref,
                       v1_ref, n1_ref, v2_ref, n2_ref, cs_ref, ci_ref, bs_ref, bi_ref,
                       gt_ref, i1t_ref, i2t_ref):
    bm = x_ref.shape[0]
    hn = _rms(x_ref[...], g_ref[...]).astype(BF16)
    hn_ref[...] = hn
    q = jnp.dot(hn, wq_ref[...], preferred_element_type=F32).astype(BF16)
    k = PEER_TOPK
    cs_ref[...] = jnp.full(cs_ref.shape, -jnp.inf, F32)
    ci_ref[...] = jnp.zeros(ci_ref.shape, F32)
    for h in range(PEER_HEADS):
        qh = q[:, h * PEER_DKEY:(h + 1) * PEER_DKEY]
        for p, (v_ref, n_ref) in enumerate(((v1_ref, n1_ref), (v2_ref, n2_ref))):
            s = lax.dot_general(keys_ref[2 * h + p], qh, (((1,), (1,)), ((), ())),
                                preferred_element_type=F32)
            _top16_rows(s, v_ref, n_ref)
        for c, (a, b) in enumerate(_PEER_CAND):
            cs_ref[pl.ds(c, 1), :] = v1_ref[pl.ds(a, 1), :] + v2_ref[pl.ds(b, 1), :]
            ci_ref[pl.ds(c, 1), :] = n1_ref[pl.ds(a, 1), :] * PEER_KEYS + n2_ref[pl.ds(b, 1), :]
        _top16_rows(cs_ref[...], bs_ref, bi_ref, payload=ci_ref[...])
        best = bs_ref[...]
        e = jnp.exp(best - best[0:1, :])
        gt_ref[pl.ds(h * k, k), :] = e / jnp.sum(e, axis=0, keepdims=True)
        flat = bi_ref[...]
        first = jnp.floor(flat * (1.0 / PEER_KEYS))
        i1t_ref[pl.ds(h * k, k), :] = first
        i2t_ref[pl.ds(h * k, k), :] = flat - first * PEER_KEYS
    gate_ref[...] = gt_ref[...].T
    i1_ref[...] = i1t_ref[...].T
    i2_ref[...] = i2t_ref[...].T


def peer_route(x, g, wq, keys, bm):
    t, d = x.shape
    slots = PEER_HEADS * PEER_TOPK
    row = lambda width: pl.BlockSpec((bm, width), lambda i: (i, 0))
    tk = lambda: pltpu.VMEM((PEER_TOPK, bm), F32)
    tall = lambda rows: pltpu.VMEM((rows, bm), F32)
    return pl.pallas_call(
        _peer_route_kernel,
        grid=(t // bm,),
        in_specs=[row(d), pl.BlockSpec((1, d), lambda i: (0, 0)),
                  pl.BlockSpec(wq.shape, lambda i: (0, 0)),
                  pl.BlockSpec(keys.shape, lambda i: (0, 0, 0))],
        out_specs=[row(d), row(slots), row(slots), row(slots)],
        out_shape=[jax.ShapeDtypeStruct((t, d), BF16)] + [jax.ShapeDtypeStruct((t, slots), F32)] * 3,
        scratch_shapes=[tk(), tk(), tk(), tk(), tall(_PEER_CAND_ROWS), tall(_PEER_CAND_ROWS),
                        tk(), tk(), tall(slots), tall(slots), tall(slots)],
        compiler_params=_params("parallel"),
        name="peer_route",
    )(x, g.reshape(1, d), wq, keys)


G_PITCH = 136
G_TOKENS_PER_MATMUL = 16


def _peer_gates_kernel(gate_ref, i1_ref, i2_ref, o_ref, s_ref):
    bm = gate_ref.shape[0]
    nk = PEER_KEYS
    slots = gate_ref.shape[2]
    sub = G_TOKENS_PER_MATMUL
    key = lax.broadcasted_iota(jnp.int32, (sub, nk, slots), 1).astype(F32)

    def build(c, carry):
        tok = pl.ds(pl.multiple_of(c * sub, sub), sub)
        wa = jnp.where(i1_ref[tok] == key, gate_ref[tok], 0.0).astype(BF16)
        wb = jnp.where(i2_ref[tok] == key, 1.0, 0.0).astype(BF16)
        g = lax.dot_general(wa, wb, (((2,), (2,)), ((0,), (0,))),
                            preferred_element_type=F32)
        for u in range(sub):
            s_ref[pl.ds(pl.multiple_of((c * sub + u) * G_PITCH, 8), nk), :] = g[u]
        return carry

    lax.fori_loop(0, bm // sub, build, 0, unroll=True)
    for i in range(nk):
        o_ref[:, i * nk:(i + 1) * nk] = s_ref[pl.ds(i, bm, stride=G_PITCH), :].astype(o_ref.dtype)


def peer_gates(gate, i1, i2, bm):
    t, slots = gate.shape
    n_exp = PEER_KEYS * PEER_KEYS
    spec = pl.BlockSpec((bm, 1, slots), lambda i: (i, 0, 0))
    r3 = lambda a: a.reshape(t, 1, slots)
    return pl.pallas_call(
        _peer_gates_kernel,
        grid=(t // bm,),
        in_specs=[spec, spec, spec],
        out_specs=pl.BlockSpec((bm, n_exp), lambda i: (i, 0)),
        out_shape=jax.ShapeDtypeStruct((t, n_exp), BF16),
        scratch_shapes=[pltpu.VMEM((bm * G_PITCH, PEER_KEYS), F32)],
        compiler_params=_params("parallel"),
        name="peer_gates",
    )(r3(gate), r3(i1), r3(i2))


def _peer_expert_kernel(x_ref, hn_ref, gm_ref, ut_ref, v_ref, o_ref):
    @pl.when(pl.program_id(1) == 0)
    def _():
        o_ref[...] = x_ref[...]

    a = jnp.dot(hn_ref[...], ut_ref[...], preferred_element_type=F32)
    act = 0.5 * a * (1.0 + lax.erf(a * (2.0 ** -0.5)))
    p = (act * gm_ref[...].astype(F32)).astype(BF16)
    o_ref[...] += jnp.dot(p, v_ref[...], preferred_element_type=F32)


def peer_experts(x, hn, gm, ut, v, bm, bn):
    t, d = x.shape
    n_exp = ut.shape[1]
    once = pl.Buffered(1)
    return pl.pallas_call(
        _peer_expert_kernel,
        grid=(t // bm, n_exp // bn),
        in_specs=[pl.BlockSpec((bm, d), lambda i, j: (i, 0), pipeline_mode=once),
                  pl.BlockSpec((bm, d), lambda i, j: (i, 0), pipeline_mode=once),
                  pl.BlockSpec((bm, bn), lambda i, j: (i, j)),
                  pl.BlockSpec((d, bn), lambda i, j: (0, j)),
                  pl.BlockSpec((bn, d), lambda i, j: (j, 0))],
        out_specs=pl.BlockSpec((bm, d), lambda i, j: (i, 0)),
        out_shape=jax.ShapeDtypeStruct((t, d), F32),
        compiler_params=_params("parallel", "arbitrary"),
        name="peer_experts",
    )(x, hn, gm, ut, v)


def _final_norm_kernel(x_ref, g_ref, o_ref):
    o_ref[...] = _rms(x_ref[...], g_ref[...])


def final_norm(x, g, bm):
    t, d = x.shape
    return pl.pallas_call(
        _final_norm_kernel,
        grid=(t // bm,),
        in_specs=[pl.BlockSpec((bm, d), lambda i: (i, 0)), pl.BlockSpec((1, d), lambda i: (0, 0))],
        out_specs=pl.BlockSpec((bm, d), lambda i: (i, 0)),
        out_shape=jax.ShapeDtypeStruct((t, d), F32),
        compiler_params=_params("parallel"),
        name="final_norm",
    )(x, g.reshape(1, d))


def _pad_heads(w, heads, dim, axis):
    shape = w.shape[:axis] + (heads, dim) + w.shape[axis + 1:]
    w = w.reshape(shape)
    pad = [(0, 0)] * w.ndim
    pad[axis + 1] = (0, HEAD_PAD - dim)
    w = jnp.pad(w, pad)
    return w.reshape(w.shape[:axis] + (heads * HEAD_PAD,) + w.shape[axis + 2:])


def _rot_half_cols(w):
    half = w.shape[-1] // 2
    return jnp.concatenate([-w[..., half:], w[..., :half]], axis=-1)


def _pack_w_in(w_in):
    rq, rk, rv, rg, qa, kva, kr, sq, sk, sv, gate = jnp.split(
        w_in, [int(p) for p in np.cumsum(
            [512, 512, 512, 512, MLA_Q_RANK, MLA_KV_RANK, MLA_ROPE,
             SWA_HEADS * SWA_DIM, SWA_KV_HEADS * SWA_DIM, SWA_KV_HEADS * SWA_DIM])], axis=-1)
    place = lambda w: jnp.pad(w, ((0, 0), (0, 0), (MLA_NOPE, HEAD_PAD - MLA_NOPE - MLA_ROPE)))
    cols = [rq, rk, rv, rg, _pad_heads(sq, SWA_HEADS, SWA_DIM, 2), gate, qa,
            _pad_heads(sk, SWA_KV_HEADS, SWA_DIM, 2), _pad_heads(sv, SWA_KV_HEADS, SWA_DIM, 2),
            kva, place(kr), place(_rot_half_cols(kr))]
    out = jnp.concatenate(cols, axis=-1).astype(BF16)
    assert out.shape[-1] == Z_COLS
    return out


def _rope_tables(seq, half):
    inv_freq = jnp.power(ROPE_BASE, -jnp.arange(half, dtype=F32) / half)
    ang = jnp.arange(seq, dtype=F32)[:, None] * inv_freq[None, :]
    return jnp.cos(ang), jnp.sin(ang)


def _t5_buckets(rel):
    n = np.maximum(rel, 0)
    max_exact = REL_BUCKETS // 2
    large = max_exact + (np.log(np.maximum(n, 1) / max_exact) / np.log(REL_MAX_DIST / max_exact)
                         * (REL_BUCKETS - max_exact)).astype(np.int32)
    large = np.minimum(large, REL_BUCKETS - 1)
    return np.where(n < max_exact, n, large).astype(np.int32)


def _swa_bias(rel_bias):
    w = SWA_WINDOW
    rel = (np.arange(w)[:, None] + w) - np.arange(2 * w)[None, :]
    bias = rel_bias[_t5_buckets(rel)].astype(F32)
    bias = jnp.where(((rel >= 0) & (rel < w))[:, :, None], bias, NEG_INF)
    bias = bias.transpose(2, 0, 1)
    return bias.reshape(SWA_KV_HEADS, (SWA_HEADS // SWA_KV_HEADS) * w, 2 * w)


def _retention_tables():
    c = RET_CHUNK
    log_gamma = np.log(1.0 - np.exp2(-5.0 - np.arange(RET_HEADS, dtype=np.float64)))
    idx = np.arange(c, dtype=np.float64)
    diff = idx[:, None] - idx[None, :]
    intra = np.where(diff >= 0, np.exp(log_gamma[:, None, None] * np.maximum(diff, 0.0)), 0.0)
    qw = np.exp(log_gamma[:, None] * (idx + 1.0)[None, :])
    kw = np.exp(log_gamma[:, None] * (c - 1.0 - idx)[None, :])
    bc = lambda a: jnp.asarray(np.broadcast_to(a[:, :, None], (RET_HEADS, c, c)), F32)
    chunk_decay = tuple(float(v) for v in np.exp(log_gamma * c))
    return jnp.asarray(intra, F32), bc(qw), bc(kw), chunk_decay


def kernel(x, mem, rel_bias, g_mix, w_in, mla_q_norm, w_mla_qb, mla_kv_norm, w_mla_kvb, swa_sinks,
           w_branch_ret, w_branch_mla, w_branch_swa, w_out, g_xattn, g_mem, w_xq, w_xkv, w_xo,
           g_ffn, w_peer_query, peer_sub_keys, peer_u, peer_v, g_final):
    batch, seq, d = x.shape
    depth = w_in.shape[0]
    t = batch * seq
    mem_len = mem.shape[1]
    bm = min(512, seq)
    big = min(1024, seq)

    w_in_p = _pack_w_in(w_in)
    qb = w_mla_qb.reshape(depth, MLA_Q_RANK, MLA_HEADS, MLA_NOPE + MLA_ROPE)
    q_nope, q_rope = qb[..., :MLA_NOPE], qb[..., MLA_NOPE:]
    zpad = jnp.zeros(qb.shape[:3] + (HEAD_PAD - MLA_NOPE - MLA_ROPE,), F32)
    flat = lambda w: w.reshape(w.shape[:2] + (MLA_HEADS * HEAD_PAD,)).astype(BF16)
    w_q1 = flat(jnp.concatenate([q_nope, q_rope, zpad], axis=-1))
    w_q2 = flat(jnp.concatenate([jnp.zeros_like(q_nope), _rot_half_cols(q_rope), zpad], axis=-1))
    kvb = w_mla_kvb.reshape(depth, MLA_KV_RANK, MLA_HEADS, MLA_NOPE + MLA_V)
    w_k = flat(jnp.pad(kvb[..., :MLA_NOPE], ((0, 0),) * 3 + ((0, HEAD_PAD - MLA_NOPE),)))
    w_v = flat(jnp.pad(kvb[..., MLA_NOPE:], ((0, 0),) * 3 + ((0, HEAD_PAD - MLA_V),)))
    w_bret = w_branch_ret.astype(BF16)
    w_bmla = _pad_heads(w_branch_mla, MLA_HEADS, MLA_V, 1).astype(BF16)
    w_bswa = _pad_heads(w_branch_swa, SWA_HEADS, SWA_DIM, 1).astype(BF16)
    w_out_b = w_out.astype(BF16)
    w_xq_b, w_xkv_b, w_xo_b = w_xq.astype(BF16), w_xkv.astype(BF16), w_xo.astype(BF16)
    w_pq = w_peer_query.astype(BF16)
    half = PEER_DKEY // 2
    keys = jnp.stack([jnp.pad(peer_sub_keys[:, :, 0], ((0, 0),) * 3 + ((0, half),)),
                      jnp.pad(peer_sub_keys[:, :, 1], ((0, 0),) * 3 + ((half, 0),))], axis=2)
    keys = keys.reshape(depth, 2 * PEER_HEADS, PEER_KEYS, PEER_DKEY).astype(BF16)
    peer_ut = jnp.swapaxes(peer_u, 1, 2).astype(BF16)
    peer_vb = peer_v.astype(BF16)

    cos64, sin64 = _rope_tables(seq, RET_DIM // 2)
    ret_cos = jnp.concatenate([cos64, cos64], axis=1)
    ret_sin = jnp.concatenate([-sin64, sin64], axis=1)
    cos16, sin16 = _rope_tables(seq, MLA_ROPE // 2)
    tail = jnp.zeros((seq, HEAD_PAD - MLA_NOPE - MLA_ROPE), F32)
    mla_cos = jnp.concatenate([jnp.ones((seq, MLA_NOPE), F32), cos16, cos16, tail], axis=1)
    mla_sin = jnp.concatenate([jnp.zeros((seq, MLA_NOPE), F32), sin16, sin16, tail], axis=1)
    intra, ret_qw, ret_kw, chunk_decay = _retention_tables()
    swa_bias = _swa_bias(rel_bias)
    grp = SWA_HEADS // SWA_KV_HEADS

    x = x.reshape(t, d)
    mem2 = mem.reshape(batch * mem_len, d)
    for l in range(depth):
        z = norm_matmul(x, g_mix[l], w_in_p[l], bm=big, bn=Z_COLS // 3, out_dtype=BF16)
        y_ret = retention(z, batch, seq, ret_cos, ret_sin, intra, ret_qw, ret_kw, chunk_decay)
        q = mla_q(z, mla_q_norm[l], w_q1[l], w_q2[l], mla_cos, mla_sin, seq, bm)
        k, v = mla_kv(z, mla_kv_norm[l], w_k[l], w_v[l], mla_cos, mla_sin, seq, bm)
        y_mla = mla_attention(q, k, v, batch, seq, bq=big)
        sinks = jnp.broadcast_to(swa_sinks[l].reshape(SWA_KV_HEADS, grp, 1, 1),
                                 (SWA_KV_HEADS, grp, SWA_WINDOW, 1)).reshape(SWA_KV_HEADS, grp * SWA_WINDOW, 1)
        y_swa = swa(z, batch, seq, swa_bias, sinks, n_win=min(4, seq // SWA_WINDOW))
        x = merge(x, y_ret, y_mla, y_swa, z, w_bret[l], w_bmla[l], w_bswa[l], w_out_b[l], bm)

        memkv = norm_matmul(mem2, g_mem[l], w_xkv_b[l], bm=min(512, batch * mem_len),
                            bn=w_xkv_b.shape[2], out_dtype=BF16).reshape(batch, mem_len, -1)
        x = xattn(x, g_xattn[l], w_xq_b[l], memkv, w_xo_b[l], seq, bm)

        hn, gate, i1, i2 = peer_route(x, g_ffn[l], w_pq[l], keys[l], bm=min(256, seq))
        gm = peer_gates(gate, i1, i2, bm=min(128, seq))
        x = peer_experts(x, hn, gm, peer_ut[l], peer_vb[l], bm=big, bn=2048)
    return final_norm(x, g_final, bm).reshape(batch, seq, d)
```

```python
import functools
import math

import numpy as np
import jax
import jax.numpy as jnp
from jax import lax
from jax.experimental import pallas as pl
from jax.experimental.pallas import tpu as pltpu

F32 = jnp.float32
BF16 = jnp.bfloat16

EPS = 1e-6
NEG_INF = -1e30
ROPE_BASE = 10000.0
N_BRANCH = 3

RET_HEADS = 4
RET_DIM = 128
RET_CHUNK = 128

MLA_HEADS = 8
MLA_NOPE = 64
MLA_ROPE = 32
MLA_V = 64
MLA_Q_RANK = 256
MLA_KV_RANK = 128

SWA_HEADS = 8
SWA_KV_HEADS = 2
SWA_DIM = 64
SWA_WINDOW = 128
REL_BUCKETS = 32
REL_MAX_DIST = 128

XA_HEADS = 4
XA_DIM = 128

PEER_HEADS = 8
PEER_KEYS = 128
PEER_DKEY = 128
PEER_TOPK = 16

LANES = 128
HEAD_PAD = LANES

Z_RQ, Z_RK, Z_RV, Z_RG = 0, 512, 1024, 1536
Z_SQ = 2048
Z_GATE = 3072
Z_QA = 6144
Z_SK = 6400
Z_SV = 6656
Z_KVA = 6912
Z_KRP = 7040
Z_KRR = 7168
Z_COLS = 7296

VMEM_LIMIT = 56 * 1024 * 1024


def _params(*sem):
    return pltpu.CompilerParams(dimension_semantics=sem, vmem_limit_bytes=VMEM_LIMIT)


def _rms(x, g=None):
    y = x * lax.rsqrt(jnp.mean(x * x, axis=-1, keepdims=True) + EPS)
    return y if g is None else y * g


def _norm_matmul_kernel(x_ref, g_ref, w_ref, o_ref, h_ref):
    @pl.when(pl.program_id(1) == 0)
    def _():
        h_ref[...] = _rms(x_ref[...].astype(F32), g_ref[...]).astype(BF16)

    o_ref[...] = jnp.dot(h_ref[...], w_ref[...], preferred_element_type=F32).astype(o_ref.dtype)


def norm_matmul(x, g, w, bm, bn, out_dtype):
    m, k = x.shape
    n = w.shape[1]
    return pl.pallas_call(
        _norm_matmul_kernel,
        grid=(m // bm, n // bn),
        in_specs=[pl.BlockSpec((bm, k), lambda i, j: (i, 0)),
                  pl.BlockSpec((1, k), lambda i, j: (0, 0)),
                  pl.BlockSpec((k, bn), lambda i, j: (0, j))],
        out_specs=pl.BlockSpec((bm, bn), lambda i, j: (i, j)),
        out_shape=jax.ShapeDtypeStruct((m, n), out_dtype),
        scratch_shapes=[pltpu.VMEM((bm, k), BF16)],
        compiler_params=_params("parallel", "arbitrary"),
        name="norm_matmul",
    )(x, g.reshape(1, k), w)


def _retention_kernel(q_ref, k_ref, v_ref, g_ref, cos_ref, sin_ref, intra_ref, qw_ref, kw_ref,
                      o_ref, state_ref, *, chunk_decay):
    @pl.when(pl.program_id(1) == 0)
    def _():
        state_ref[...] = jnp.zeros_like(state_ref)

    cos = cos_ref[...]
    sin = sin_ref[...]
    half = RET_DIM // 2

    def rope(t):
        return t * cos + pltpu.roll(t, half, 1) * sin

    for h in range(RET_HEADS):
        cols = slice(h * RET_DIM, (h + 1) * RET_DIM)
        q = rope(q_ref[:, cols].astype(F32)).astype(BF16)
        k = rope(k_ref[:, cols].astype(F32)) * (RET_DIM ** -0.5)
        v = v_ref[:, cols]
        state = state_ref[h]
        scores = lax.dot_general(q, k.astype(BF16), (((1,), (1,)), ((), ())),
                                 preferred_element_type=F32) * intra_ref[h]
        inner = jnp.dot(scores.astype(BF16), v, preferred_element_type=F32)
        cross = jnp.dot(q, state.astype(BF16), preferred_element_type=F32) * qw_ref[h]
        kv = lax.dot_general((k * kw_ref[h]).astype(BF16), v, (((0,), (0,)), ((), ())),
                             preferred_element_type=F32)
        state_ref[h] = chunk_decay[h] * state + kv
        y = _rms(inner + cross)
        gate = g_ref[:, cols].astype(F32)
        o_ref[:, cols] = (y * (gate * jax.nn.sigmoid(gate))).astype(o_ref.dtype)


def retention(z, batch, seq, cos, sin, intra, qw, kw, chunk_decay):
    c = RET_CHUNK
    nc = seq // c
    w = RET_HEADS * RET_DIM
    zspec = lambda col: pl.BlockSpec((c, w), lambda b, n, col=col: (b * nc + n, col // w))
    tab = pl.BlockSpec((c, RET_DIM), lambda b, n: (n, 0))
    const = pl.BlockSpec((RET_HEADS, c, c), lambda b, n: (0, 0, 0))
    return pl.pallas_call(
        functools.partial(_retention_kernel, chunk_decay=chunk_decay),
        grid=(batch, nc),
        in_specs=[zspec(Z_RQ), zspec(Z_RK), zspec(Z_RV), zspec(Z_RG), tab, tab, const, const, const],
        out_specs=pl.BlockSpec((c, w), lambda b, n: (b * nc + n, 0)),
        out_shape=jax.ShapeDtypeStruct((batch * seq, w), BF16),
        scratch_shapes=[pltpu.VMEM((RET_HEADS, RET_DIM, RET_DIM), F32)],
        compiler_params=_params("parallel", "arbitrary"),
        name="retention",
    )(z, z, z, z, cos, sin, intra, qw, kw)


def _mla_q_kernel(qa_ref, g_ref, w1_ref, w2_ref, cos_ref, sin_ref, o_ref):
    qn = _rms(qa_ref[...].astype(F32), g_ref[...]).astype(BF16)
    a = jnp.dot(qn, w1_ref[...], preferred_element_type=F32)
    b = jnp.dot(qn, w2_ref[...], preferred_element_type=F32)
    cos = cos_ref[...]
    sin = sin_ref[...]
    scale = (MLA_NOPE + MLA_ROPE) ** -0.5 * math.log2(math.e)
    for h in range(MLA_HEADS):
        cols = slice(h * HEAD_PAD, (h + 1) * HEAD_PAD)
        o_ref[:, cols] = ((a[:, cols] * cos + b[:, cols] * sin) * scale).astype(o_ref.dtype)


def mla_q(z, g, w1, w2, cos, sin, seq, bm):
    t = z.shape[0]
    wd = MLA_HEADS * HEAD_PAD
    nsb = seq // bm
    return pl.pallas_call(
        _mla_q_kernel,
        grid=(t // bm,),
        in_specs=[pl.BlockSpec((bm, MLA_Q_RANK), lambda i: (i, Z_QA // MLA_Q_RANK)),
                  pl.BlockSpec((1, MLA_Q_RANK), lambda i: (0, 0)),
                  pl.BlockSpec((MLA_Q_RANK, wd), lambda i: (0, 0)),
                  pl.BlockSpec((MLA_Q_RANK, wd), lambda i: (0, 0)),
                  pl.BlockSpec((bm, HEAD_PAD), lambda i: (i % nsb, 0)),
                  pl.BlockSpec((bm, HEAD_PAD), lambda i: (i % nsb, 0))],
        out_specs=pl.BlockSpec((bm, wd), lambda i: (i, 0)),
        out_shape=jax.ShapeDtypeStruct((t, wd), BF16),
        compiler_params=_params("parallel"),
        name="mla_q",
    )(z, g.reshape(1, -1), w1, w2, cos, sin)


def _mla_kv_kernel(kva_ref, krp_ref, krr_ref, g_ref, wk_ref, wv_ref, cos_ref, sin_ref, k_ref, v_ref):
    kvn = _rms(kva_ref[...].astype(F32), g_ref[...]).astype(BF16)
    k = jnp.dot(kvn, wk_ref[...], preferred_element_type=F32)
    krope = krp_ref[...].astype(F32) * cos_ref[...] + krr_ref[...].astype(F32) * sin_ref[...]
    for h in range(MLA_HEADS):
        cols = slice(h * HEAD_PAD, (h + 1) * HEAD_PAD)
        k_ref[:, cols] = (k[:, cols] + krope).astype(k_ref.dtype)
    v_ref[...] = jnp.dot(kvn, wv_ref[...], preferred_element_type=F32).astype(v_ref.dtype)


def mla_kv(z, g, wk, wv, cos, sin, seq, bm):
    t = z.shape[0]
    wd = MLA_HEADS * HEAD_PAD
    nsb = seq // bm
    zs = lambda col: pl.BlockSpec((bm, LANES), lambda i, col=col: (i, col // LANES))
    tab = pl.BlockSpec((bm, HEAD_PAD), lambda i: (i % nsb, 0))
    wspec = pl.BlockSpec((MLA_KV_RANK, wd), lambda i: (0, 0))
    out = pl.BlockSpec((bm, wd), lambda i: (i, 0))
    return pl.pallas_call(
        _mla_kv_kernel,
        grid=(t // bm,),
        in_specs=[zs(Z_KVA), zs(Z_KRP), zs(Z_KRR), pl.BlockSpec((1, MLA_KV_RANK), lambda i: (0, 0)),
                  wspec, wspec, tab, tab],
        out_specs=[out, out],
        out_shape=[jax.ShapeDtypeStruct((t, wd), BF16)] * 2,
        compiler_params=_params("parallel"),
        name="mla_kv",
    )(z, z, z, g.reshape(1, -1), wk, wv, cos, sin)


def _mla_attn_kernel(q_ref, k_ref, v_ref, o_ref, *, bq, n_split, heads):
    qi = pl.program_id(2)
    rows = bq // n_split
    chains = [(h, r) for h in range(heads) for r in range(n_split)]
    cols = lambda h: slice(h * HEAD_PAD, (h + 1) * HEAD_PAD)
    qs = [q_ref[pl.ds(r * rows, rows), cols(h)] for h, r in chains]

    def step(j, carry, diagonal):
        start = pl.multiple_of(j * bq, bq)
        out = []
        for c, (h, r) in enumerate(chains):
            m, l, acc = carry[c]
            n_keys = (r + 1) * rows if diagonal else bq
            kb = k_ref[pl.ds(start, n_keys), cols(h)]
            vb = v_ref[pl.ds(start, n_keys), cols(h)]
            s = lax.dot_general(qs[c], kb, (((1,), (1,)), ((), ())), preferred_element_type=F32)
            if diagonal:
                row = r * rows + lax.broadcasted_iota(jnp.int32, (rows, n_keys), 0)
                col = lax.broadcasted_iota(jnp.int32, (rows, n_keys), 1)
                s = jnp.where(col <= row, s, NEG_INF)
            m_new = jnp.maximum(m, jnp.max(s, axis=-1, keepdims=True))
            alpha = jnp.exp2(m - m_new)
            p = jnp.exp2(s - m_new)
            l = alpha * l + jnp.sum(p, axis=-1, keepdims=True)
            acc = alpha * acc + jnp.dot(p.astype(BF16), vb, preferred_element_type=F32)
            out.append((m_new, l, acc))
        return tuple(out)

    init = tuple((jnp.full((rows, 1), NEG_INF, F32), jnp.zeros((rows, 1), F32),
                  jnp.zeros((rows, HEAD_PAD), F32)) for _ in chains)
    carry = lax.fori_loop(0, qi, functools.partial(step, diagonal=False), init)
    carry = step(qi, carry, diagonal=True)
    for c, (h, r) in enumerate(chains):
        _, l, acc = carry[c]
        o_ref[pl.ds(r * rows, rows), cols(h)] = (acc / l).astype(o_ref.dtype)


def mla_attention(q, k, v, batch, seq, bq, n_split=2, heads=1):
    nq = seq // bq
    width = heads * HEAD_PAD
    qspec = pl.BlockSpec((bq, width), lambda b, h, i: (b * nq + i, h))
    kvspec = pl.BlockSpec((seq, width), lambda b, h, i: (b, h))
    return pl.pallas_call(
        functools.partial(_mla_attn_kernel, bq=bq, n_split=n_split, heads=heads),
        grid=(batch, MLA_HEADS // heads, nq),
        in_specs=[qspec, kvspec, kvspec],
        out_specs=qspec,
        out_shape=jax.ShapeDtypeStruct(q.shape, BF16),
        compiler_params=_params("parallel", "parallel", "arbitrary"),
        name="mla_attention",
    )(q, k, v)


def _swa_kernel(q_ref, kc_ref, kp_ref, vc_ref, vp_ref, bias_ref, sink_ref, o_ref, *, n_win):
    w = SWA_WINDOW
    grp = SWA_HEADS // SWA_KV_HEADS
    first = pl.program_id(1) == 0
    prev_half = lax.broadcasted_iota(jnp.int32, (grp * w, 2 * w), 1) < w
    for r in range(n_win):
        rows = slice(r * w, (r + 1) * w)
        for g in range(SWA_KV_HEADS):
            kcols = slice(g * HEAD_PAD, (g + 1) * HEAD_PAD)
            if r == 0:
                kb = jnp.concatenate([kp_ref[:, kcols], kc_ref[0:w, kcols]], axis=0)
                vb = jnp.concatenate([vp_ref[:, kcols], vc_ref[0:w, kcols]], axis=0)
            else:
                kb = kc_ref[(r - 1) * w:(r + 1) * w, kcols]
                vb = vc_ref[(r - 1) * w:(r + 1) * w, kcols]
            q = jnp.concatenate(
                [q_ref[rows, (g * grp + j) * HEAD_PAD:(g * grp + j + 1) * HEAD_PAD] for j in range(grp)], axis=0)
            s = lax.dot_general(q, kb, (((1,), (1,)), ((), ())), preferred_element_type=F32)
            s = s * (SWA_DIM ** -0.5) + bias_ref[g]
            if r == 0:
                s = jnp.where(jnp.logical_and(first, prev_half), NEG_INF, s)
            sink = sink_ref[g]
            m = jnp.maximum(jnp.max(s, axis=-1, keepdims=True), sink)
            e = jnp.exp(s - m)
            p = e / (jnp.sum(e, axis=-1, keepdims=True) + jnp.exp(sink - m))
            out = jnp.dot(p.astype(BF16), vb, preferred_element_type=F32)
            for j in range(grp):
                o_ref[rows, (g * grp + j) * HEAD_PAD:(g * grp + j + 1) * HEAD_PAD] = (
                    out[j * w:(j + 1) * w].astype(o_ref.dtype))


def swa(z, batch, seq, bias, sinks, n_win):
    w = SWA_WINDOW
    rows = n_win * w
    ns = seq // rows
    qw = SWA_HEADS * HEAD_PAD
    kw = SWA_KV_HEADS * HEAD_PAD
    grp = SWA_HEADS // SWA_KV_HEADS
    cur = lambda col: pl.BlockSpec((rows, kw), lambda b, n, col=col: (b * ns + n, col // kw))
    prev = lambda col: pl.BlockSpec(
        (w, kw), lambda b, n, col=col: ((b * ns + n) * n_win - jnp.minimum(n, 1), col // kw))
    return pl.pallas_call(
        functools.partial(_swa_kernel, n_win=n_win),
        grid=(batch, ns),
        in_specs=[pl.BlockSpec((rows, qw), lambda b, n: (b * ns + n, Z_SQ // qw)),
                  cur(Z_SK), prev(Z_SK), cur(Z_SV), prev(Z_SV),
                  pl.BlockSpec((SWA_KV_HEADS, grp * w, 2 * w), lambda b, n: (0, 0, 0)),
                  pl.BlockSpec((SWA_KV_HEADS, grp * w, 1), lambda b, n: (0, 0, 0))],
        out_specs=pl.BlockSpec((rows, qw), lambda b, n: (b * ns + n, 0)),
        out_shape=jax.ShapeDtypeStruct((batch * seq, qw), BF16),
        compiler_params=_params("parallel", "arbitrary"),
        name="swa",
    )(z, z, z, z, z, bias, sinks)


def _merge_xattn_kernel(x_ref, yr_ref, ym_ref, ys_ref, gl_ref, wr_ref, wm_ref, ws_ref, wo_ref,
                        g_ref, wq_ref, kv_ref, wxo_ref, o_ref):
    d = x_ref.shape[1]
    merged = None
    for i, (y_ref, w_ref) in enumerate(((yr_ref, wr_ref), (ym_ref, wm_ref), (ys_ref, ws_ref))):
        gate = jax.nn.sigmoid(gl_ref[:, i * d:(i + 1) * d].astype(F32))
        term = gate * jnp.dot(y_ref[...], w_ref[...], preferred_element_type=F32)
        merged = term if merged is None else merged + term
    x = x_ref[...] + jnp.dot(merged.astype(BF16), wo_ref[...], preferred_element_type=F32)

    hn = _rms(x, g_ref[...]).astype(BF16)
    q = (jnp.dot(hn, wq_ref[...], preferred_element_type=F32) * (XA_DIM ** -0.5)).astype(BF16)
    width = XA_HEADS * XA_DIM
    outs = []
    for h in range(XA_HEADS):
        cols = slice(h * XA_DIM, (h + 1) * XA_DIM)
        k = kv_ref[0, :, cols]
        v = kv_ref[0, :, width + h * XA_DIM:width + (h + 1) * XA_DIM]
        s = lax.dot_general(q[:, cols], k, (((1,), (1,)), ((), ())), preferred_element_type=F32)
        e = jnp.exp(s - jnp.max(s, axis=-1, keepdims=True))
        p = e / jnp.sum(e, axis=-1, keepdims=True)
        outs.append(jnp.dot(p.astype(BF16), v, preferred_element_type=F32).astype(BF16))
    o = jnp.concatenate(outs, axis=1)
    o_ref[...] = x + jnp.dot(o, wxo_ref[...], preferred_element_type=F32)


def merge_xattn(x, y_ret, y_mla, y_swa, z, wr, wm, ws, wo, g, wq, memkv, wxo, seq, bm):
    t, d = x.shape
    nsb = seq // bm
    row = lambda width: pl.BlockSpec((bm, width), lambda i: (i, 0))
    full = lambda a: pl.BlockSpec(a.shape, lambda i: (0, 0))
    return pl.pallas_call(
        _merge_xattn_kernel,
        grid=(t // bm,),
        in_specs=[row(d), row(y_ret.shape[1]), row(y_mla.shape[1]), row(y_swa.shape[1]),
                  pl.BlockSpec((bm, N_BRANCH * d), lambda i: (i, Z_GATE // (N_BRANCH * d))),
                  full(wr), full(wm), full(ws), full(wo),
                  pl.BlockSpec((1, d), lambda i: (0, 0)), full(wq),
                  pl.BlockSpec((1,) + memkv.shape[1:], lambda i: (i // nsb, 0, 0)), full(wxo)],
        out_specs=row(d),
        out_shape=jax.ShapeDtypeStruct((t, d), F32),
        compiler_params=_params("parallel"),
        name="merge_xattn",
    )(x, y_ret, y_mla, y_swa, z, wr, wm, ws, wo, g.reshape(1, d), wq, memkv, wxo)


_PEER_CAND = [(a, b) for a in range(PEER_TOPK) for b in range(PEER_TOPK) if (a + 1) * (b + 1) <= PEER_TOPK]
_PEER_NCAND = len(_PEER_CAND)
_PEER_CAND_ROWS = -(-_PEER_NCAND // 8) * 8


def _top16_rows(s, vals_ref, idx_ref, payload=None):
    n, width = s.shape
    rows = lax.broadcasted_iota(jnp.int32, (n, LANES), 0).astype(F32)
    for c in range(width // LANES):
        lanes = slice(c * LANES, (c + 1) * LANES)
        sc = s[:, lanes]
        pc = None if payload is None else payload[:, lanes]
        for r in range(PEER_TOPK):
            m = jnp.max(sc, axis=0, keepdims=True)
            win = jnp.min(jnp.where(sc == m, rows, float(n)), axis=0, keepdims=True)
            hit = rows == win
            vals_ref[pl.ds(r, 1), lanes] = m
            if pc is None:
                idx_ref[pl.ds(r, 1), lanes] = win
            else:
                idx_ref[pl.ds(r, 1), lanes] = jnp.sum(jnp.where(hit, pc, 0.0), axis=0, keepdims=True)
            sc = jnp.where(hit, -jnp.inf, sc)


def _peer_route_kernel(x_ref, g_ref, wq_ref, keys_ref, hn_ref, gate_ref, i1_ref, i2_ref,
                       v1_ref, n1_ref, v2_ref, n2_ref, cs_ref, ci_ref, bs_ref, bi_ref,
                       gt_ref, i1t_ref, i2t_ref):
    bm = x_ref.shape[0]
    hn = _rms(x_ref[...], g_ref[...]).astype(BF16)
    hn_ref[...] = hn
    q = jnp.dot(hn, wq_ref[...], preferred_element_type=F32).astype(BF16)
    k = PEER_TOPK
    cs_ref[...] = jnp.full(cs_ref.shape, -jnp.inf, F32)
    ci_ref[...] = jnp.zeros(ci_ref.shape, F32)
    for h in range(PEER_HEADS):
        qh = q[:, h * PEER_DKEY:(h + 1) * PEER_DKEY]
        for p, (v_ref, n_ref) in enumerate(((v1_ref, n1_ref), (v2_ref, n2_ref))):
            s = lax.dot_general(keys_ref[2 * h + p], qh, (((1,), (1,)), ((), ())),
                                preferred_element_type=F32)
            _top16_rows(s, v_ref, n_ref)
        for c, (a, b) in enumerate(_PEER_CAND):
            cs_ref[pl.ds(c, 1), :] = v1_ref[pl.ds(a, 1), :] + v2_ref[pl.ds(b, 1), :]
            ci_ref[pl.ds(c, 1), :] = n1_ref[pl.ds(a, 1), :] * PEER_KEYS + n2_ref[pl.ds(b, 1), :]
        _top16_rows(cs_ref[...], bs_ref, bi_ref, payload=ci_ref[...])
        best = bs_ref[...]
        e = jnp.exp(best - best[0:1, :])
        gt_ref[pl.ds(h * k, k), :] = e / jnp.sum(e, axis=0, keepdims=True)
        flat = bi_ref[...]
        first = jnp.floor(flat * (1.0 / PEER_KEYS))
        i1t_ref[pl.ds(h * k, k), :] = first
        i2t_ref[pl.ds(h * k, k), :] = flat - first * PEER_KEYS
    gate_ref[...] = gt_ref[...].T
    i1_ref[...] = i1t_ref[...].T
    i2_ref[...] = i2t_ref[...].T


def peer_route(x, g, wq, keys, bm):
    t, d = x.shape
    slots = PEER_HEADS * PEER_TOPK
    row = lambda width: pl.BlockSpec((bm, width), lambda i: (i, 0))
    tk = lambda: pltpu.VMEM((PEER_TOPK, bm), F32)
    tall = lambda rows: pltpu.VMEM((rows, bm), F32)
    return pl.pallas_call(
        _peer_route_kernel,
        grid=(t // bm,),
        in_specs=[row(d), pl.BlockSpec((1, d), lambda i: (0, 0)),
                  pl.BlockSpec(wq.shape, lambda i: (0, 0)),
                  pl.BlockSpec(keys.shape, lambda i: (0, 0, 0))],
        out_specs=[row(d), row(slots), row(slots), row(slots)],
        out_shape=[jax.ShapeDtypeStruct((t, d), BF16)] + [jax.ShapeDtypeStruct((t, slots), F32)] * 3,
        scratch_shapes=[tk(), tk(), tk(), tk(), tall(_PEER_CAND_ROWS), tall(_PEER_CAND_ROWS),
                        tk(), tk(), tall(slots), tall(slots), tall(slots)],
        compiler_params=_params("parallel"),
        name="peer_route",
    )(x, g.reshape(1, d), wq, keys)


G_PITCH = 132
G_TOKENS_PER_MATMUL = 16
G_COPY_GROUP = 32


def _peer_gates_kernel(gate_ref, i1_ref, i2_ref, o_ref, *s_refs):
    nk = PEER_KEYS
    slots = gate_ref.shape[2]
    sub = G_TOKENS_PER_MATMUL
    grp = G_COPY_GROUP
    key = lax.broadcasted_iota(jnp.int32, (sub, nk, slots), 1).astype(F32)

    def build(gi):
        for c in range(grp // sub):
            tok = pl.ds(gi * grp + c * sub, sub)
            wa = jnp.where(i1_ref[tok] == key, gate_ref[tok], 0.0).astype(BF16)
            wb = jnp.where(i2_ref[tok] == key, 1.0, 0.0).astype(BF16)
            g = lax.dot_general(wa, wb, (((2,), (2,)), ((0,), (0,))),
                                preferred_element_type=F32)
            for u in range(sub):
                s_refs[gi][pl.ds((c * sub + u) * G_PITCH, nk), :] = g[u]

    def gather(gi):
        for i in range(nk):
            o_ref[gi * grp:(gi + 1) * grp, i * nk:(i + 1) * nk] = (
                s_refs[gi][pl.ds(i, grp, stride=G_PITCH), :].astype(o_ref.dtype))

    for gi in range(len(s_refs)):
        build(gi)
        if gi:
            gather(gi - 1)
    gather(len(s_refs) - 1)


def peer_gates(gate, i1, i2, bm):
    t, slots = gate.shape
    n_exp = PEER_KEYS * PEER_KEYS
    spec = pl.BlockSpec((bm, 1, slots), lambda i: (i, 0, 0))
    r3 = lambda a: a.reshape(t, 1, slots)
    return pl.pallas_call(
        _peer_gates_kernel,
        grid=(t // bm,),
        in_specs=[spec, spec, spec],
        out_specs=pl.BlockSpec((bm, n_exp), lambda i: (i, 0)),
        out_shape=jax.ShapeDtypeStruct((t, n_exp), BF16),
        scratch_shapes=[pltpu.VMEM((G_COPY_GROUP * G_PITCH, PEER_KEYS), F32)] * (bm // G_COPY_GROUP),
        compiler_params=_params("parallel"),
        name="peer_gates",
    )(r3(gate), r3(i1), r3(i2))


def _peer_expert_kernel(x_ref, hn_ref, gm_ref, ut_ref, v_ref, o_ref):
    @pl.when(pl.program_id(1) == 0)
    def _():
        o_ref[...] = x_ref[...]

    a = jnp.dot(hn_ref[...], ut_ref[...], preferred_element_type=F32)
    act = 0.5 * a * (1.0 + lax.erf(a * (2.0 ** -0.5)))
    p = (act * gm_ref[...].astype(F32)).astype(BF16)
    o_ref[...] += jnp.dot(p, v_ref[...], preferred_element_type=F32)


def peer_experts(x, hn, gm, ut, v, bm, bn):
    t, d = x.shape
    n_exp = ut.shape[1]
    once = pl.Buffered(1)
    return pl.pallas_call(
        _peer_expert_kernel,
        grid=(t // bm, n_exp // bn),
        in_specs=[pl.BlockSpec((bm, d), lambda i, j: (i, 0), pipeline_mode=once),
                  pl.BlockSpec((bm, d), lambda i, j: (i, 0)),
                  pl.BlockSpec((bm, bn), lambda i, j: (i, j)),
                  pl.BlockSpec((d, bn), lambda i, j: (0, j)),
                  pl.BlockSpec((bn, d), lambda i, j: (j, 0))],
        out_specs=pl.BlockSpec((bm, d), lambda i, j: (i, 0)),
        out_shape=jax.ShapeDtypeStruct((t, d), F32),
        compiler_params=_params("parallel", "arbitrary"),
        name="peer_experts",
    )(x, hn, gm, ut, v)


def _final_norm_kernel(x_ref, g_ref, o_ref):
    o_ref[...] = _rms(x_ref[...], g_ref[...])


def final_norm(x, g, bm):
    t, d = x.shape
    return pl.pallas_call(
        _final_norm_kernel,
        grid=(t // bm,),
        in_specs=[pl.BlockSpec((bm, d), lambda i: (i, 0)), pl.BlockSpec((1, d), lambda i: (0, 0))],
        out_specs=pl.BlockSpec((bm, d), lambda i: (i, 0)),
        out_shape=jax.ShapeDtypeStruct((t, d), F32),
        compiler_params=_params("parallel"),
        name="final_norm",
    )(x, g.reshape(1, d))


def _pad_heads(w, heads, dim, axis):
    shape = w.shape[:axis] + (heads, dim) + w.shape[axis + 1:]
    w = w.reshape(shape)
    pad = [(0, 0)] * w.ndim
    pad[axis + 1] = (0, HEAD_PAD - dim)
    w = jnp.pad(w, pad)
    return w.reshape(w.shape[:axis] + (heads * HEAD_PAD,) + w.shape[axis + 2:])


def _rot_half_cols(w):
    half = w.shape[-1] // 2
    return jnp.concatenate([-w[..., half:], w[..., :half]], axis=-1)


def _pack_w_in(w_in):
    rq, rk, rv, rg, qa, kva, kr, sq, sk, sv, gate = jnp.split(
        w_in, [int(p) for p in np.cumsum(
            [512, 512, 512, 512, MLA_Q_RANK, MLA_KV_RANK, MLA_ROPE,
             SWA_HEADS * SWA_DIM, SWA_KV_HEADS * SWA_DIM, SWA_KV_HEADS * SWA_DIM])], axis=-1)
    place = lambda w: jnp.pad(w, ((0, 0), (0, 0), (MLA_NOPE, HEAD_PAD - MLA_NOPE - MLA_ROPE)))
    cols = [rq, rk, rv, rg, _pad_heads(sq, SWA_HEADS, SWA_DIM, 2), gate, qa,
            _pad_heads(sk, SWA_KV_HEADS, SWA_DIM, 2), _pad_heads(sv, SWA_KV_HEADS, SWA_DIM, 2),
            kva, place(kr), place(_rot_half_cols(kr))]
    out = jnp.concatenate(cols, axis=-1).astype(BF16)
    assert out.shape[-1] == Z_COLS
    return out


def _rope_tables(seq, half):
    inv_freq = jnp.power(ROPE_BASE, -jnp.arange(half, dtype=F32) / half)
    ang = jnp.arange(seq, dtype=F32)[:, None] * inv_freq[None, :]
    return jnp.cos(ang), jnp.sin(ang)


def _t5_buckets(rel):
    n = np.maximum(rel, 0)
    max_exact = REL_BUCKETS // 2
    large = max_exact + (np.log(np.maximum(n, 1) / max_exact) / np.log(REL_MAX_DIST / max_exact)
                         * (REL_BUCKETS - max_exact)).astype(np.int32)
    large = np.minimum(large, REL_BUCKETS - 1)
    return np.where(n < max_exact, n, large).astype(np.int32)


def _swa_bias(rel_bias):
    w = SWA_WINDOW
    rel = (np.arange(w)[:, None] + w) - np.arange(2 * w)[None, :]
    bucket = _t5_buckets(rel)
    table = rel_bias.astype(F32).T
    bias = sum(jnp.where(bucket[None] == b, table[:, b][:, None, None], 0.0) for b in range(REL_BUCKETS))
    bias = jnp.where(((rel >= 0) & (rel < w))[None], bias, NEG_INF)
    return bias.reshape(SWA_KV_HEADS, (SWA_HEADS // SWA_KV_HEADS) * w, 2 * w)


def _retention_tables():
    c = RET_CHUNK
    log_gamma = np.log(1.0 - np.exp2(-5.0 - np.arange(RET_HEADS, dtype=np.float64)))
    idx = np.arange(c, dtype=np.float64)
    diff = idx[:, None] - idx[None, :]
    intra = np.where(diff >= 0, np.exp(log_gamma[:, None, None] * np.maximum(diff, 0.0)), 0.0)
    qw = np.exp(log_gamma[:, None] * (idx + 1.0)[None, :])
    kw = np.exp(log_gamma[:, None] * (c - 1.0 - idx)[None, :])
    bc = lambda a: jnp.asarray(np.broadcast_to(a[:, :, None], (RET_HEADS, c, c)), F32)
    chunk_decay = tuple(float(v) for v in np.exp(log_gamma * c))
    return jnp.asarray(intra, F32), bc(qw), bc(kw), chunk_decay


def kernel(x, mem, rel_bias, g_mix, w_in, mla_q_norm, w_mla_qb, mla_kv_norm, w_mla_kvb, swa_sinks,
           w_branch_ret, w_branch_mla, w_branch_swa, w_out, g_xattn, g_mem, w_xq, w_xkv, w_xo,
           g_ffn, w_peer_query, peer_sub_keys, peer_u, peer_v, g_final):
    batch, seq, d = x.shape
    depth = w_in.shape[0]
    t = batch * seq
    mem_len = mem.shape[1]
    bm = min(512, seq)
    big = min(1024, seq)

    w_in_p = _pack_w_in(w_in)
    qb = w_mla_qb.reshape(depth, MLA_Q_RANK, MLA_HEADS, MLA_NOPE + MLA_ROPE)
    q_nope, q_rope = qb[..., :MLA_NOPE], qb[..., MLA_NOPE:]
    zpad = jnp.zeros(qb.shape[:3] + (HEAD_PAD - MLA_NOPE - MLA_ROPE,), F32)
    flat = lambda w: w.reshape(w.shape[:2] + (MLA_HEADS * HEAD_PAD,)).astype(BF16)
    w_q1 = flat(jnp.concatenate([q_nope, q_rope, zpad], axis=-1))
    w_q2 = flat(jnp.concatenate([jnp.zeros_like(q_nope), _rot_half_cols(q_rope), zpad], axis=-1))
    kvb = w_mla_kvb.reshape(depth, MLA_KV_RANK, MLA_HEADS, MLA_NOPE + MLA_V)
    w_k = flat(jnp.pad(kvb[..., :MLA_NOPE], ((0, 0),) * 3 + ((0, HEAD_PAD - MLA_NOPE),)))
    w_v = flat(jnp.pad(kvb[..., MLA_NOPE:], ((0, 0),) * 3 + ((0, HEAD_PAD - MLA_V),)))
    w_bret = w_branch_ret.astype(BF16)
    w_bmla = _pad_heads(w_branch_mla, MLA_HEADS, MLA_V, 1).astype(BF16)
    w_bswa = _pad_heads(w_branch_swa, SWA_HEADS, SWA_DIM, 1).astype(BF16)
    w_out_b = w_out.astype(BF16)
    w_xq_b, w_xkv_b, w_xo_b = w_xq.astype(BF16), w_xkv.astype(BF16), w_xo.astype(BF16)
    w_pq = w_peer_query.astype(BF16)
    half = PEER_DKEY // 2
    keys = jnp.stack([jnp.pad(peer_sub_keys[:, :, 0], ((0, 0),) * 3 + ((0, half),)),
                      jnp.pad(peer_sub_keys[:, :, 1], ((0, 0),) * 3 + ((half, 0),))], axis=2)
    keys = keys.reshape(depth, 2 * PEER_HEADS, PEER_KEYS, PEER_DKEY).astype(BF16)
    peer_ut = jnp.swapaxes(peer_u, 1, 2).astype(BF16)
    peer_vb = peer_v.astype(BF16)

    cos64, sin64 = _rope_tables(seq, RET_DIM // 2)
    ret_cos = jnp.concatenate([cos64, cos64], axis=1)
    ret_sin = jnp.concatenate([-sin64, sin64], axis=1)
    cos16, sin16 = _rope_tables(seq, MLA_ROPE // 2)
    tail = jnp.zeros((seq, HEAD_PAD - MLA_NOPE - MLA_ROPE), F32)
    mla_cos = jnp.concatenate([jnp.ones((seq, MLA_NOPE), F32), cos16, cos16, tail], axis=1)
    mla_sin = jnp.concatenate([jnp.zeros((seq, MLA_NOPE), F32), sin16, sin16, tail], axis=1)
    intra, ret_qw, ret_kw, chunk_decay = _retention_tables()
    swa_bias = _swa_bias(rel_bias)
    grp = SWA_HEADS // SWA_KV_HEADS

    x = x.reshape(t, d)
    mem2 = mem.reshape(batch * mem_len, d)
    for l in range(depth):
        z = norm_matmul(x, g_mix[l], w_in_p[l], bm=big, bn=Z_COLS // 3, out_dtype=BF16)
        y_ret = retention(z, batch, seq, ret_cos, ret_sin, intra, ret_qw, ret_kw, chunk_decay)
        q = mla_q(z, mla_q_norm[l], w_q1[l], w_q2[l], mla_cos, mla_sin, seq, bm)
        k, v = mla_kv(z, mla_kv_norm[l], w_k[l], w_v[l], mla_cos, mla_sin, seq, bm)
        y_mla = mla_attention(q, k, v, batch, seq, bq=big)
        sinks = jnp.broadcast_to(swa_sinks[l].reshape(SWA_KV_HEADS, grp, 1, 1),
                                 (SWA_KV_HEADS, grp, SWA_WINDOW, 1)).reshape(SWA_KV_HEADS, grp * SWA_WINDOW, 1)
        y_swa = swa(z, batch, seq, swa_bias, sinks, n_win=min(4, seq // SWA_WINDOW))
        memkv = norm_matmul(mem2, g_mem[l], w_xkv_b[l], bm=min(512, batch * mem_len),
                            bn=w_xkv_b.shape[2], out_dtype=BF16).reshape(batch, mem_len, -1)
        x = merge_xattn(x, y_ret, y_mla, y_swa, z, w_bret[l], w_bmla[l], w_bswa[l], w_out_b[l],
                        g_xattn[l], w_xq_b[l], memkv, w_xo_b[l], seq, bm)

        hn, gate, i1, i2 = peer_route(x, g_ffn[l], w_pq[l], keys[l], bm=min(256, seq))
        gm = peer_gates(gate, i1, i2, bm=min(128, seq))
        x = peer_experts(x, hn, gm, peer_ut[l], peer_vb[l], bm=big, bn=2048)
    return final_norm(x, g_final, bm).reshape(batch, seq, d)
```

```python
import functools
import math

import numpy as np
import jax
import jax.numpy as jnp
from jax import lax
from jax.experimental import pallas as pl
from jax.experimental.pallas import tpu as pltpu

F32 = jnp.float32
BF16 = jnp.bfloat16

EPS = 1e-6
NEG_INF = -1e30
ROPE_BASE = 10000.0
N_BRANCH = 3

RET_HEADS = 4
RET_DIM = 128
RET_CHUNK = 128

MLA_HEADS = 8
MLA_NOPE = 64
MLA_ROPE = 32
MLA_V = 64
MLA_Q_RANK = 256
MLA_KV_RANK = 128

SWA_HEADS = 8
SWA_KV_HEADS = 2
SWA_DIM = 64
SWA_WINDOW = 128
REL_BUCKETS = 32
REL_MAX_DIST = 128

XA_HEADS = 4
XA_DIM = 128

PEER_HEADS = 8
PEER_KEYS = 128
PEER_DKEY = 128
PEER_TOPK = 16

LANES = 128
HEAD_PAD = LANES

Z_RQ, Z_RK, Z_RV, Z_RG = 0, 512, 1024, 1536
Z_SQ = 2048
Z_GATE = 3072
Z_QA = 6144
Z_SK = 6400
Z_SV = 6656
Z_KVA = 6912
Z_KRP = 7040
Z_KRR = 7168
Z_COLS = 7296

VMEM_LIMIT = 56 * 1024 * 1024


def _params(*sem):
    return pltpu.CompilerParams(dimension_semantics=sem, vmem_limit_bytes=VMEM_LIMIT)


def _rms(x, g=None):
    y = x * lax.rsqrt(jnp.mean(x * x, axis=-1, keepdims=True) + EPS)
    return y if g is None else y * g


def _norm_matmul_kernel(x_ref, g_ref, w_ref, o_ref, h_ref):
    @pl.when(pl.program_id(1) == 0)
    def _():
        h_ref[...] = _rms(x_ref[...].astype(F32), g_ref[...]).astype(BF16)

    o_ref[...] = jnp.dot(h_ref[...], w_ref[...], preferred_element_type=F32).astype(o_ref.dtype)


def norm_matmul(x, g, w, bm, bn, out_dtype):
    m, k = x.shape
    n = w.shape[1]
    return pl.pallas_call(
        _norm_matmul_kernel,
        grid=(m // bm, n // bn),
        in_specs=[pl.BlockSpec((bm, k), lambda i, j: (i, 0)),
                  pl.BlockSpec((1, k), lambda i, j: (0, 0)),
                  pl.BlockSpec((k, bn), lambda i, j: (0, j))],
        out_specs=pl.BlockSpec((bm, bn), lambda i, j: (i, j)),
        out_shape=jax.ShapeDtypeStruct((m, n), out_dtype),
        scratch_shapes=[pltpu.VMEM((bm, k), BF16)],
        compiler_params=_params("parallel", "arbitrary"),
        name="norm_matmul",
    )(x, g.reshape(1, k), w)


def _retention_kernel(q_ref, k_ref, v_ref, g_ref, cos_ref, sin_ref, intra_ref, qw_ref, kw_ref,
                      o_ref, state_ref, *, chunk_decay):
    @pl.when(pl.program_id(1) == 0)
    def _():
        state_ref[...] = jnp.zeros_like(state_ref)

    half = RET_DIM // 2
    c = RET_CHUNK
    for h in range(RET_HEADS):
        cols = slice(h * RET_DIM, (h + 1) * RET_DIM)
        state = state_ref[h]
        for n in range(q_ref.shape[0] // c):
            rows = slice(n * c, (n + 1) * c)
            cos = cos_ref[rows, :]
            sin = sin_ref[rows, :]
            rope = lambda t: t * cos + pltpu.roll(t, half, 1) * sin
            q = rope(q_ref[rows, cols].astype(F32)).astype(BF16)
            k = rope(k_ref[rows, cols].astype(F32)) * (RET_DIM ** -0.5)
            v = v_ref[rows, cols]
            scores = lax.dot_general(q, k.astype(BF16), (((1,), (1,)), ((), ())),
                                     preferred_element_type=F32) * intra_ref[h]
            inner = jnp.dot(scores.astype(BF16), v, preferred_element_type=F32)
            cross = jnp.dot(q, state.astype(BF16), preferred_element_type=F32) * qw_ref[h]
            kv = lax.dot_general((k * kw_ref[h]).astype(BF16), v, (((0,), (0,)), ((), ())),
                                 preferred_element_type=F32)
            state = chunk_decay[h] * state + kv
            y = _rms(inner + cross)
            gate = g_ref[rows, cols].astype(F32)
            o_ref[rows, cols] = (y * (gate * jax.nn.sigmoid(gate))).astype(o_ref.dtype)
        state_ref[h] = state


def retention(z, batch, seq, cos, sin, intra, qw, kw, chunk_decay, chunks_per_step):
    c = RET_CHUNK
    rows = c * chunks_per_step
    nc = seq // rows
    w = RET_HEADS * RET_DIM
    zspec = lambda col: pl.BlockSpec((rows, w), lambda b, n, col=col: (b * nc + n, col // w))
    tab = pl.BlockSpec((rows, RET_DIM), lambda b, n: (n, 0))
    const = pl.BlockSpec((RET_HEADS, c, c), lambda b, n: (0, 0, 0))
    return pl.pallas_call(
        functools.partial(_retention_kernel, chunk_decay=chunk_decay),
        grid=(batch, nc),
        in_specs=[zspec(Z_RQ), zspec(Z_RK), zspec(Z_RV), zspec(Z_RG), tab, tab, const, const, const],
        out_specs=pl.BlockSpec((rows, w), lambda b, n: (b * nc + n, 0)),
        out_shape=jax.ShapeDtypeStruct((batch * seq, w), BF16),
        scratch_shapes=[pltpu.VMEM((RET_HEADS, RET_DIM, RET_DIM), F32)],
        compiler_params=_params("parallel", "arbitrary"),
        name="retention",
    )(z, z, z, z, cos, sin, intra, qw, kw)


def _mla_q_kernel(qa_ref, g_ref, w1_ref, w2_ref, cos_ref, sin_ref, o_ref):
    qn = _rms(qa_ref[...].astype(F32), g_ref[...]).astype(BF16)
    a = jnp.dot(qn, w1_ref[...], preferred_element_type=F32)
    b = jnp.dot(qn, w2_ref[...], preferred_element_type=F32)
    cos = cos_ref[...]
    sin = sin_ref[...]
    scale = (MLA_NOPE + MLA_ROPE) ** -0.5 * math.log2(math.e)
    for h in range(MLA_HEADS):
        cols = slice(h * HEAD_PAD, (h + 1) * HEAD_PAD)
        o_ref[:, cols] = ((a[:, cols] * cos + b[:, cols] * sin) * scale).astype(o_ref.dtype)


def mla_q(z, g, w1, w2, cos, sin, seq, bm):
    t = z.shape[0]
    wd = MLA_HEADS * HEAD_PAD
    nsb = seq // bm
    return pl.pallas_call(
        _mla_q_kernel,
        grid=(t // bm,),
        in_specs=[pl.BlockSpec((bm, MLA_Q_RANK), lambda i: (i, Z_QA // MLA_Q_RANK)),
                  pl.BlockSpec((1, MLA_Q_RANK), lambda i: (0, 0)),
                  pl.BlockSpec((MLA_Q_RANK, wd), lambda i: (0, 0)),
                  pl.BlockSpec((MLA_Q_RANK, wd), lambda i: (0, 0)),
                  pl.BlockSpec((bm, HEAD_PAD), lambda i: (i % nsb, 0)),
                  pl.BlockSpec((bm, HEAD_PAD), lambda i: (i % nsb, 0))],
        out_specs=pl.BlockSpec((bm, wd), lambda i: (i, 0)),
        out_shape=jax.ShapeDtypeStruct((t, wd), BF16),
        compiler_params=_params("parallel"),
        name="mla_q",
    )(z, g.reshape(1, -1), w1, w2, cos, sin)


def _mla_kv_kernel(kva_ref, krp_ref, krr_ref, g_ref, wk_ref, wv_ref, cos_ref, sin_ref, k_ref, v_ref):
    kvn = _rms(kva_ref[...].astype(F32), g_ref[...]).astype(BF16)
    k = jnp.dot(kvn, wk_ref[...], preferred_element_type=F32)
    krope = krp_ref[...].astype(F32) * cos_ref[...] + krr_ref[...].astype(F32) * sin_ref[...]
    for h in range(MLA_HEADS):
        cols = slice(h * HEAD_PAD, (h + 1) * HEAD_PAD)
        k_ref[:, cols] = (k[:, cols] + krope).astype(k_ref.dtype)
    v_ref[...] = jnp.dot(kvn, wv_ref[...], preferred_element_type=F32).astype(v_ref.dtype)


def mla_kv(z, g, wk, wv, cos, sin, seq, bm):
    t = z.shape[0]
    wd = MLA_HEADS * HEAD_PAD
    nsb = seq // bm
    zs = lambda col: pl.BlockSpec((bm, LANES), lambda i, col=col: (i, col // LANES))
    tab = pl.BlockSpec((bm, HEAD_PAD), lambda i: (i % nsb, 0))
    wspec = pl.BlockSpec((MLA_KV_RANK, wd), lambda i: (0, 0))
    out = pl.BlockSpec((bm, wd), lambda i: (i, 0))
    return pl.pallas_call(
        _mla_kv_kernel,
        grid=(t // bm,),
        in_specs=[zs(Z_KVA), zs(Z_KRP), zs(Z_KRR), pl.BlockSpec((1, MLA_KV_RANK), lambda i: (0, 0)),
                  wspec, wspec, tab, tab],
        out_specs=[out, out],
        out_shape=[jax.ShapeDtypeStruct((t, wd), BF16)] * 2,
        compiler_params=_params("parallel"),
        name="mla_kv",
    )(z, z, z, g.reshape(1, -1), wk, wv, cos, sin)


def _mla_attn_kernel(q_ref, k_ref, v_ref, o_ref, *, bq, n_split, heads):
    qi = pl.program_id(2)
    rows = bq // n_split
    chains = [(h, r) for h in range(heads) for r in range(n_split)]
    cols = lambda h: slice(h * HEAD_PAD, (h + 1) * HEAD_PAD)
    qs = [q_ref[pl.ds(r * rows, rows), cols(h)] for h, r in chains]

    def step(j, carry, diagonal):
        start = pl.multiple_of(j * bq, bq)
        out = []
        for c, (h, r) in enumerate(chains):
            m, l, acc = carry[c]
            n_keys = (r + 1) * rows if diagonal else bq
            kb = k_ref[pl.ds(start, n_keys), cols(h)]
            vb = v_ref[pl.ds(start, n_keys), cols(h)]
            s = lax.dot_general(qs[c], kb, (((1,), (1,)), ((), ())), preferred_element_type=F32)
            if diagonal:
                row = r * rows + lax.broadcasted_iota(jnp.int32, (rows, n_keys), 0)
                col = lax.broadcasted_iota(jnp.int32, (rows, n_keys), 1)
                s = jnp.where(col <= row, s, NEG_INF)
            m_new = jnp.maximum(m, jnp.max(s, axis=-1, keepdims=True))
            alpha = jnp.exp2(m - m_new)
            p = jnp.exp2(s - m_new)
            l = alpha * l + jnp.sum(p, axis=-1, keepdims=True)
            acc = alpha * acc + jnp.dot(p.astype(BF16), vb, preferred_element_type=F32)
            out.append((m_new, l, acc))
        return tuple(out)

    init = tuple((jnp.full((rows, 1), NEG_INF, F32), jnp.zeros((rows, 1), F32),
                  jnp.zeros((rows, HEAD_PAD), F32)) for _ in chains)
    carry = lax.fori_loop(0, qi, functools.partial(step, diagonal=False), init)
    carry = step(qi, carry, diagonal=True)
    per_block = HEAD_PAD // MLA_V
    for r in range(n_split):
        outs = []
        for h in range(heads):
            _, l, acc = carry[chains.index((h, r))]
            outs.append((acc / l)[:, :MLA_V])
        for b in range(heads // per_block):
            o_ref[pl.ds(r * rows, rows), cols(b)] = jnp.concatenate(
                outs[b * per_block:(b + 1) * per_block], axis=1).astype(o_ref.dtype)


def mla_attention(q, k, v, batch, seq, bq, n_split=2, heads=2):
    nq = seq // bq
    width = heads * HEAD_PAD
    qspec = pl.BlockSpec((bq, width), lambda b, h, i: (b * nq + i, h))
    kvspec = pl.BlockSpec((seq, width), lambda b, h, i: (b, h))
    return pl.pallas_call(
        functools.partial(_mla_attn_kernel, bq=bq, n_split=n_split, heads=heads),
        grid=(batch, MLA_HEADS // heads, nq),
        in_specs=[qspec, kvspec, kvspec],
        out_specs=pl.BlockSpec((bq, heads * MLA_V), lambda b, h, i: (b * nq + i, h)),
        out_shape=jax.ShapeDtypeStruct((q.shape[0], MLA_HEADS * MLA_V), BF16),
        compiler_params=_params("parallel", "parallel", "arbitrary"),
        name="mla_attention",
    )(q, k, v)


def _swa_kernel(q_ref, kc_ref, kp_ref, vc_ref, vp_ref, bias_ref, sink_ref, o_ref, *, n_win):
    w = SWA_WINDOW
    grp = SWA_HEADS // SWA_KV_HEADS
    first = pl.program_id(1) == 0
    prev_half = lax.broadcasted_iota(jnp.int32, (grp * w, 2 * w), 1) < w
    for r in range(n_win):
        rows = slice(r * w, (r + 1) * w)
        for g in range(SWA_KV_HEADS):
            kcols = slice(g * HEAD_PAD, (g + 1) * HEAD_PAD)
            if r == 0:
                kb = jnp.concatenate([kp_ref[:, kcols], kc_ref[0:w, kcols]], axis=0)
                vb = jnp.concatenate([vp_ref[:, kcols], vc_ref[0:w, kcols]], axis=0)
            else:
                kb = kc_ref[(r - 1) * w:(r + 1) * w, kcols]
                vb = vc_ref[(r - 1) * w:(r + 1) * w, kcols]
            q = jnp.concatenate(
                [q_ref[rows, (g * grp + j) * HEAD_PAD:(g * grp + j + 1) * HEAD_PAD] for j in range(grp)], axis=0)
            s = lax.dot_general(q, kb, (((1,), (1,)), ((), ())), preferred_element_type=F32)
            s = s * (SWA_DIM ** -0.5) + bias_ref[g]
            if r == 0:
                s = jnp.where(jnp.logical_and(first, prev_half), NEG_INF, s)
            sink = sink_ref[g]
            m = jnp.maximum(jnp.max(s, axis=-1, keepdims=True), sink)
            e = jnp.exp(s - m)
            p = e / (jnp.sum(e, axis=-1, keepdims=True) + jnp.exp(sink - m))
            out = jnp.dot(p.astype(BF16), vb, preferred_element_type=F32)
            for j in range(grp):
                o_ref[rows, (g * grp + j) * HEAD_PAD:(g * grp + j + 1) * HEAD_PAD] = (
                    out[j * w:(j + 1) * w].astype(o_ref.dtype))


def swa(z, batch, seq, bias, sinks, n_win):
    w = SWA_WINDOW
    rows = n_win * w
    ns = seq // rows
    qw = SWA_HEADS * HEAD_PAD
    kw = SWA_KV_HEADS * HEAD_PAD
    grp = SWA_HEADS // SWA_KV_HEADS
    cur = lambda col: pl.BlockSpec((rows, kw), lambda b, n, col=col: (b * ns + n, col // kw))
    prev = lambda col: pl.BlockSpec(
        (w, kw), lambda b, n, col=col: ((b * ns + n) * n_win - jnp.minimum(n, 1), col // kw))
    return pl.pallas_call(
        functools.partial(_swa_kernel, n_win=n_win),
        grid=(batch, ns),
        in_specs=[pl.BlockSpec((rows, qw), lambda b, n: (b * ns + n, Z_SQ // qw)),
                  cur(Z_SK), prev(Z_SK), cur(Z_SV), prev(Z_SV),
                  pl.BlockSpec((SWA_KV_HEADS, grp * w, 2 * w), lambda b, n: (0, 0, 0)),
                  pl.BlockSpec((SWA_KV_HEADS, grp * w, 1), lambda b, n: (0, 0, 0))],
        out_specs=pl.BlockSpec((rows, qw), lambda b, n: (b * ns + n, 0)),
        out_shape=jax.ShapeDtypeStruct((batch * seq, qw), BF16),
        compiler_params=_params("parallel", "arbitrary"),
        name="swa",
    )(z, z, z, z, z, bias, sinks)


def _merge_xattn_kernel(x_ref, yr_ref, ym_ref, ys_ref, gl_ref, wr_ref, wm_ref, ws_ref, wo_ref,
                        g_ref, wq_ref, kv_ref, wxo_ref, o_ref):
    d = x_ref.shape[1]
    merged = None
    for i, (y_ref, w_ref) in enumerate(((yr_ref, wr_ref), (ym_ref, wm_ref), (ys_ref, ws_ref))):
        gate = jax.nn.sigmoid(gl_ref[:, i * d:(i + 1) * d].astype(F32))
        term = gate * jnp.dot(y_ref[...], w_ref[...], preferred_element_type=F32)
        merged = term if merged is None else merged + term
    x = x_ref[...] + jnp.dot(merged.astype(BF16), wo_ref[...], preferred_element_type=F32)

    hn = _rms(x, g_ref[...]).astype(BF16)
    q = (jnp.dot(hn, wq_ref[...], preferred_element_type=F32) * (XA_DIM ** -0.5)).astype(BF16)
    width = XA_HEADS * XA_DIM
    outs = []
    for h in range(XA_HEADS):
        cols = slice(h * XA_DIM, (h + 1) * XA_DIM)
        k = kv_ref[0, :, cols]
        v = kv_ref[0, :, width + h * XA_DIM:width + (h + 1) * XA_DIM]
        s = lax.dot_general(q[:, cols], k, (((1,), (1,)), ((), ())), preferred_element_type=F32)
        e = jnp.exp(s - jnp.max(s, axis=-1, keepdims=True))
        p = e / jnp.sum(e, axis=-1, keepdims=True)
        outs.append(jnp.dot(p.astype(BF16), v, preferred_element_type=F32).astype(BF16))
    o = jnp.concatenate(outs, axis=1)
    o_ref[...] = x + jnp.dot(o, wxo_ref[...], preferred_element_type=F32)


def merge_xattn(x, y_ret, y_mla, y_swa, z, wr, wm, ws, wo, g, wq, memkv, wxo, seq, bm):
    t, d = x.shape
    nsb = seq // bm
    row = lambda width: pl.BlockSpec((bm, width), lambda i: (i, 0))
    full = lambda a: pl.BlockSpec(a.shape, lambda i: (0, 0))
    return pl.pallas_call(
        _merge_xattn_kernel,
        grid=(t // bm,),
        in_specs=[row(d), row(y_ret.shape[1]), row(y_mla.shape[1]), row(y_swa.shape[1]),
                  pl.BlockSpec((bm, N_BRANCH * d), lambda i: (i, Z_GATE // (N_BRANCH * d))),
                  full(wr), full(wm), full(ws), full(wo),
                  pl.BlockSpec((1, d), lambda i: (0, 0)), full(wq),
                  pl.BlockSpec((1,) + memkv.shape[1:], lambda i: (i // nsb, 0, 0)), full(wxo)],
        out_specs=row(d),
        out_shape=jax.ShapeDtypeStruct((t, d), F32),
        compiler_params=_params("parallel"),
        name="merge_xattn",
    )(x, y_ret, y_mla, y_swa, z, wr, wm, ws, wo, g.reshape(1, d), wq, memkv, wxo)


_PEER_CAND = [(a, b) for a in range(PEER_TOPK) for b in range(PEER_TOPK) if (a + 1) * (b + 1) <= PEER_TOPK]
_PEER_NCAND = len(_PEER_CAND)
_PEER_CAND_ROWS = -(-_PEER_NCAND // 8) * 8


def _top16_rows(s, vals_ref, idx_ref, payload=None):
    n, width = s.shape
    rows = lax.broadcasted_iota(jnp.int32, (n, LANES), 0).astype(F32)
    for c in range(width // LANES):
        lanes = slice(c * LANES, (c + 1) * LANES)
        sc = s[:, lanes]
        pc = None if payload is None else payload[:, lanes]
        for r in range(PEER_TOPK):
            m = jnp.max(sc, axis=0, keepdims=True)
            win = jnp.min(jnp.where(sc == m, rows, float(n)), axis=0, keepdims=True)
            hit = rows == win
            vals_ref[pl.ds(r, 1), lanes] = m
            if pc is None:
                idx_ref[pl.ds(r, 1), lanes] = win
            else:
                idx_ref[pl.ds(r, 1), lanes] = jnp.sum(jnp.where(hit, pc, 0.0), axis=0, keepdims=True)
            sc = jnp.where(hit, -jnp.inf, sc)


def _peer_route_kernel(x_ref, g_ref, wq_ref, keys_ref, hn_ref, gate_ref, i1_ref, i2_ref,
                       v1_ref, n1_ref, v2_ref, n2_ref, cs_ref, ci_ref, bs_ref, bi_ref,
                       gt_ref, i1t_ref, i2t_ref):
    bm = x_ref.shape[0]
    hn = _rms(x_ref[...], g_ref[...]).astype(BF16)
    hn_ref[...] = hn
    q = jnp.dot(hn, wq_ref[...], preferred_element_type=F32).astype(BF16)
    k = PEER_TOPK
    cs_ref[...] = jnp.full(cs_ref.shape, -jnp.inf, F32)
    ci_ref[...] = jnp.zeros(ci_ref.shape, F32)
    for h in range(PEER_HEADS):
        qh = q[:, h * PEER_DKEY:(h + 1) * PEER_DKEY]
        for p, (v_ref, n_ref) in enumerate(((v1_ref, n1_ref), (v2_ref, n2_ref))):
            s = lax.dot_general(keys_ref[2 * h + p], qh, (((1,), (1,)), ((), ())),
                                preferred_element_type=F32)
            _top16_rows(s, v_ref, n_ref)
        for c, (a, b) in enumerate(_PEER_CAND):
            cs_ref[pl.ds(c, 1), :] = v1_ref[pl.ds(a, 1), :] + v2_ref[pl.ds(b, 1), :]
            ci_ref[pl.ds(c, 1), :] = n1_ref[pl.ds(a, 1), :] * PEER_KEYS + n2_ref[pl.ds(b, 1), :]
        _top16_rows(cs_ref[...], bs_ref, bi_ref, payload=ci_ref[...])
        best = bs_ref[...]
        e = jnp.exp(best - best[0:1, :])
        gt_ref[pl.ds(h * k, k), :] = e / jnp.sum(e, axis=0, keepdims=True)
        flat = bi_ref[...]
        first = jnp.floor(flat * (1.0 / PEER_KEYS))
        i1t_ref[pl.ds(h * k, k), :] = first
        i2t_ref[pl.ds(h * k, k), :] = flat - first * PEER_KEYS
    gate_ref[...] = gt_ref[...].T
    i1_ref[...] = i1t_ref[...].T
    i2_ref[...] = i2t_ref[...].T


def peer_route(x, g, wq, keys, bm):
    t, d = x.shape
    slots = PEER_HEADS * PEER_TOPK
    row = lambda width: pl.BlockSpec((bm, width), lambda i: (i, 0))
    tk = lambda: pltpu.VMEM((PEER_TOPK, bm), F32)
    tall = lambda rows: pltpu.VMEM((rows, bm), F32)
    return pl.pallas_call(
        _peer_route_kernel,
        grid=(t // bm,),
        in_specs=[row(d), pl.BlockSpec((1, d), lambda i: (0, 0)),
                  pl.BlockSpec(wq.shape, lambda i: (0, 0)),
                  pl.BlockSpec(keys.shape, lambda i: (0, 0, 0))],
        out_specs=[row(d), row(slots), row(slots), row(slots)],
        out_shape=[jax.ShapeDtypeStruct((t, d), BF16)] + [jax.ShapeDtypeStruct((t, slots), F32)] * 3,
        scratch_shapes=[tk(), tk(), tk(), tk(), tall(_PEER_CAND_ROWS), tall(_PEER_CAND_ROWS),
                        tk(), tk(), tall(slots), tall(slots), tall(slots)],
        compiler_params=_params("parallel"),
        name="peer_route",
    )(x, g.reshape(1, d), wq, keys)


G_PITCH = 132
G_TOKENS_PER_MATMUL = 16
G_COPY_GROUP = 32


def _peer_gates_kernel(gate_ref, i1_ref, i2_ref, o_ref, *s_refs):
    nk = PEER_KEYS
    slots = gate_ref.shape[2]
    sub = G_TOKENS_PER_MATMUL
    grp = G_COPY_GROUP
    key = lax.broadcasted_iota(jnp.int32, (sub, nk, slots), 1).astype(F32)

    def build(gi):
        for c in range(grp // sub):
            tok = pl.ds(gi * grp + c * sub, sub)
            wa = jnp.where(i1_ref[tok] == key, gate_ref[tok], 0.0).astype(BF16)
            wb = jnp.where(i2_ref[tok] == key, 1.0, 0.0).astype(BF16)
            g = lax.dot_general(wa, wb, (((2,), (2,)), ((0,), (0,))),
                                preferred_element_type=F32)
            for u in range(sub):
                s_refs[gi][pl.ds((c * sub + u) * G_PITCH, nk), :] = g[u]

    def gather(gi):
        for i in range(nk):
            o_ref[gi * grp:(gi + 1) * grp, i * nk:(i + 1) * nk] = (
                s_refs[gi][pl.ds(i, grp, stride=G_PITCH), :].astype(o_ref.dtype))

    for gi in range(len(s_refs)):
        build(gi)
        if gi:
            gather(gi - 1)
    gather(len(s_refs) - 1)


def peer_gates(gate, i1, i2, bm):
    t, slots = gate.shape
    n_exp = PEER_KEYS * PEER_KEYS
    spec = pl.BlockSpec((bm, 1, slots), lambda i: (i, 0, 0))
    r3 = lambda a: a.reshape(t, 1, slots)
    return pl.pallas_call(
        _peer_gates_kernel,
        grid=(t // bm,),
        in_specs=[spec, spec, spec],
        out_specs=pl.BlockSpec((bm, n_exp), lambda i: (i, 0)),
        out_shape=jax.ShapeDtypeStruct((t, n_exp), BF16),
        scratch_shapes=[pltpu.VMEM((G_COPY_GROUP * G_PITCH, PEER_KEYS), F32)] * (bm // G_COPY_GROUP),
        compiler_params=_params("parallel"),
        name="peer_gates",
    )(r3(gate), r3(i1), r3(i2))


def _peer_expert_kernel(x_ref, hn_ref, gm_ref, ut_ref, v_ref, o_ref):
    @pl.when(pl.program_id(1) == 0)
    def _():
        o_ref[...] = x_ref[...]

    a = jnp.dot(hn_ref[...], ut_ref[...], preferred_element_type=F32)
    act = 0.5 * a * (1.0 + lax.erf(a * (2.0 ** -0.5)))
    p = (act * gm_ref[...].astype(F32)).astype(BF16)
    o_ref[...] += jnp.dot(p, v_ref[...], preferred_element_type=F32)


def peer_experts(x, hn, gm, ut, v, bm, bn):
    t, d = x.shape
    n_exp = ut.shape[1]
    once = pl.Buffered(1)
    return pl.pallas_call(
        _peer_expert_kernel,
        grid=(t // bm, n_exp // bn),
        in_specs=[pl.BlockSpec((bm, d), lambda i, j: (i, 0)),
                  pl.BlockSpec((bm, d), lambda i, j: (i, 0)),
                  pl.BlockSpec((bm, bn), lambda i, j: (i, j)),
                  pl.BlockSpec((d, bn), lambda i, j: (0, j)),
                  pl.BlockSpec((bn, d), lambda i, j: (j, 0))],
        out_specs=pl.BlockSpec((bm, d), lambda i, j: (i, 0)),
        out_shape=jax.ShapeDtypeStruct((t, d), F32),
        compiler_params=_params("parallel", "arbitrary"),
        name="peer_experts",
    )(x, hn, gm, ut, v)


def _final_norm_kernel(x_ref, g_ref, o_ref):
    o_ref[...] = _rms(x_ref[...], g_ref[...])


def final_norm(x, g, bm):
    t, d = x.shape
    return pl.pallas_call(
        _final_norm_kernel,
        grid=(t // bm,),
        in_specs=[pl.BlockSpec((bm, d), lambda i: (i, 0)), pl.BlockSpec((1, d), lambda i: (0, 0))],
        out_specs=pl.BlockSpec((bm, d), lambda i: (i, 0)),
        out_shape=jax.ShapeDtypeStruct((t, d), F32),
        compiler_params=_params("parallel"),
        name="final_norm",
    )(x, g.reshape(1, d))


def _pad_heads(w, heads, dim, axis):
    shape = w.shape[:axis] + (heads, dim) + w.shape[axis + 1:]
    w = w.reshape(shape)
    pad = [(0, 0)] * w.ndim
    pad[axis + 1] = (0, HEAD_PAD - dim)
    w = jnp.pad(w, pad)
    return w.reshape(w.shape[:axis] + (heads * HEAD_PAD,) + w.shape[axis + 2:])


def _rot_half_cols(w):
    half = w.shape[-1] // 2
    return jnp.concatenate([-w[..., half:], w[..., :half]], axis=-1)


def _pack_w_in(w_in):
    rq, rk, rv, rg, qa, kva, kr, sq, sk, sv, gate = jnp.split(
        w_in, [int(p) for p in np.cumsum(
            [512, 512, 512, 512, MLA_Q_RANK, MLA_KV_RANK, MLA_ROPE,
             SWA_HEADS * SWA_DIM, SWA_KV_HEADS * SWA_DIM, SWA_KV_HEADS * SWA_DIM])], axis=-1)
    place = lambda w: jnp.pad(w, ((0, 0), (0, 0), (MLA_NOPE, HEAD_PAD - MLA_NOPE - MLA_ROPE)))
    cols = [rq, rk, rv, rg, _pad_heads(sq, SWA_HEADS, SWA_DIM, 2), gate, qa,
            _pad_heads(sk, SWA_KV_HEADS, SWA_DIM, 2), _pad_heads(sv, SWA_KV_HEADS, SWA_DIM, 2),
            kva, place(kr), place(_rot_half_cols(kr))]
    out = jnp.concatenate(cols, axis=-1).astype(BF16)
    assert out.shape[-1] == Z_COLS
    return out


def _rope_tables(seq, half):
    inv_freq = jnp.power(ROPE_BASE, -jnp.arange(half, dtype=F32) / half)
    ang = jnp.arange(seq, dtype=F32)[:, None] * inv_freq[None, :]
    return jnp.cos(ang), jnp.sin(ang)


def _t5_buckets(rel):
    n = np.maximum(rel, 0)
    max_exact = REL_BUCKETS // 2
    large = max_exact + (np.log(np.maximum(n, 1) / max_exact) / np.log(REL_MAX_DIST / max_exact)
                         * (REL_BUCKETS - max_exact)).astype(np.int32)
    large = np.minimum(large, REL_BUCKETS - 1)
    return np.where(n < max_exact, n, large).astype(np.int32)


def _swa_bias(rel_bias):
    w = SWA_WINDOW
    rel = (np.arange(w)[:, None] + w) - np.arange(2 * w)[None, :]
    bucket = _t5_buckets(rel)
    table = rel_bias.astype(F32).T
    bias = sum(jnp.where(bucket[None] == b, table[:, b][:, None, None], 0.0) for b in range(REL_BUCKETS))
    bias = jnp.where(((rel >= 0) & (rel < w))[None], bias, NEG_INF)
    return bias.reshape(SWA_KV_HEADS, (SWA_HEADS // SWA_KV_HEADS) * w, 2 * w)


def _retention_tables():
    c = RET_CHUNK
    log_gamma = np.log(1.0 - np.exp2(-5.0 - np.arange(RET_HEADS, dtype=np.float64)))
    idx = np.arange(c, dtype=np.float64)
    diff = idx[:, None] - idx[None, :]
    intra = np.where(diff >= 0, np.exp(log_gamma[:, None, None] * np.maximum(diff, 0.0)), 0.0)
    qw = np.exp(log_gamma[:, None] * (idx + 1.0)[None, :])
    kw = np.exp(log_gamma[:, None] * (c - 1.0 - idx)[None, :])
    bc = lambda a: jnp.asarray(np.broadcast_to(a[:, :, None], (RET_HEADS, c, c)), F32)
    chunk_decay = tuple(float(v) for v in np.exp(log_gamma * c))
    return jnp.asarray(intra, F32), bc(qw), bc(kw), chunk_decay


def kernel(x, mem, rel_bias, g_mix, w_in, mla_q_norm, w_mla_qb, mla_kv_norm, w_mla_kvb, swa_sinks,
           w_branch_ret, w_branch_mla, w_branch_swa, w_out, g_xattn, g_mem, w_xq, w_xkv, w_xo,
           g_ffn, w_peer_query, peer_sub_keys, peer_u, peer_v, g_final):
    batch, seq, d = x.shape
    depth = w_in.shape[0]
    t = batch * seq
    mem_len = mem.shape[1]
    bm = min(512, seq)
    big = min(1024, seq)

    w_in_p = _pack_w_in(w_in)
    qb = w_mla_qb.reshape(depth, MLA_Q_RANK, MLA_HEADS, MLA_NOPE + MLA_ROPE)
    q_nope, q_rope = qb[..., :MLA_NOPE], qb[..., MLA_NOPE:]
    zpad = jnp.zeros(qb.shape[:3] + (HEAD_PAD - MLA_NOPE - MLA_ROPE,), F32)
    flat = lambda w: w.reshape(w.shape[:2] + (MLA_HEADS * HEAD_PAD,)).astype(BF16)
    w_q1 = flat(jnp.concatenate([q_nope, q_rope, zpad], axis=-1))
    w_q2 = flat(jnp.concatenate([jnp.zeros_like(q_nope), _rot_half_cols(q_rope), zpad], axis=-1))
    kvb = w_mla_kvb.reshape(depth, MLA_KV_RANK, MLA_HEADS, MLA_NOPE + MLA_V)
    w_k = flat(jnp.pad(kvb[..., :MLA_NOPE], ((0, 0),) * 3 + ((0, HEAD_PAD - MLA_NOPE),)))
    w_v = flat(jnp.pad(kvb[..., MLA_NOPE:], ((0, 0),) * 3 + ((0, HEAD_PAD - MLA_V),)))
    w_bret = w_branch_ret.astype(BF16)
    w_bmla = w_branch_mla.astype(BF16)
    w_bswa = _pad_heads(w_branch_swa, SWA_HEADS, SWA_DIM, 1).astype(BF16)
    w_out_b = w_out.astype(BF16)
    w_xq_b, w_xkv_b, w_xo_b = w_xq.astype(BF16), w_xkv.astype(BF16), w_xo.astype(BF16)
    w_pq = w_peer_query.astype(BF16)
    half = PEER_DKEY // 2
    keys = jnp.stack([jnp.pad(peer_sub_keys[:, :, 0], ((0, 0),) * 3 + ((0, half),)),
                      jnp.pad(peer_sub_keys[:, :, 1], ((0, 0),) * 3 + ((half, 0),))], axis=2)
    keys = keys.reshape(depth, 2 * PEER_HEADS, PEER_KEYS, PEER_DKEY).astype(BF16)
    peer_ut = jnp.swapaxes(peer_u, 1, 2).astype(BF16)
    peer_vb = peer_v.astype(BF16)

    cos64, sin64 = _rope_tables(seq, RET_DIM // 2)
    ret_cos = jnp.concatenate([cos64, cos64], axis=1)
    ret_sin = jnp.concatenate([-sin64, sin64], axis=1)
    cos16, sin16 = _rope_tables(seq, MLA_ROPE // 2)
    tail = jnp.zeros((seq, HEAD_PAD - MLA_NOPE - MLA_ROPE), F32)
    mla_cos = jnp.concatenate([jnp.ones((seq, MLA_NOPE), F32), cos16, cos16, tail], axis=1)
    mla_sin = jnp.concatenate([jnp.zeros((seq, MLA_NOPE), F32), sin16, sin16, tail], axis=1)
    intra, ret_qw, ret_kw, chunk_decay = _retention_tables()
    swa_bias = _swa_bias(rel_bias)
    grp = SWA_HEADS // SWA_KV_HEADS

    x = x.reshape(t, d)
    mem2 = mem.reshape(batch * mem_len, d)
    for l in range(depth):
        z = norm_matmul(x, g_mix[l], w_in_p[l], bm=big, bn=Z_COLS // 3, out_dtype=BF16)
        y_ret = retention(z, batch, seq, ret_cos, ret_sin, intra, ret_qw, ret_kw, chunk_decay,
                          chunks_per_step=min(4, seq // RET_CHUNK))
        q = mla_q(z, mla_q_norm[l], w_q1[l], w_q2[l], mla_cos, mla_sin, seq, bm)
        k, v = mla_kv(z, mla_kv_norm[l], w_k[l], w_v[l], mla_cos, mla_sin, seq, bm)
        y_mla = mla_attention(q, k, v, batch, seq, bq=big)
        sinks = jnp.broadcast_to(swa_sinks[l].reshape(SWA_KV_HEADS, grp, 1, 1),
                                 (SWA_KV_HEADS, grp, SWA_WINDOW, 1)).reshape(SWA_KV_HEADS, grp * SWA_WINDOW, 1)
        y_swa = swa(z, batch, seq, swa_bias, sinks, n_win=min(4, seq // SWA_WINDOW))
        memkv = norm_matmul(mem2, g_mem[l], w_xkv_b[l], bm=min(512, batch * mem_len),
                            bn=w_xkv_b.shape[2], out_dtype=BF16).reshape(batch, mem_len, -1)
        x = merge_xattn(x, y_ret, y_mla, y_swa, z, w_bret[l], w_bmla[l], w_bswa[l], w_out_b[l],
                        g_xattn[l], w_xq_b[l], memkv, w_xo_b[l], seq, bm)

        hn, gate, i1, i2 = peer_route(x, g_ffn[l], w_pq[l], keys[l], bm=min(256, seq))
        gm = peer_gates(gate, i1, i2, bm=min(256, seq))
        x = peer_experts(x, hn, gm, peer_ut[l], peer_vb[l], bm=big, bn=2048)
    return final_norm(x, g_final, bm).reshape(batch, seq, d)
```

```python
import functools
import math

import numpy as np
import jax
import jax.numpy as jnp
from jax import lax
from jax.experimental import pallas as pl
from jax.experimental.pallas import tpu as pltpu

F32 = jnp.float32
BF16 = jnp.bfloat16

EPS = 1e-6
NEG_INF = -1e30
ROPE_BASE = 10000.0
N_BRANCH = 3

RET_HEADS = 4
RET_DIM = 128
RET_CHUNK = 128

MLA_HEADS = 8
MLA_NOPE = 64
MLA_ROPE = 32
MLA_V = 64
MLA_Q_RANK = 256
MLA_KV_RANK = 128

SWA_HEADS = 8
SWA_KV_HEADS = 2
SWA_DIM = 64
SWA_WINDOW = 128
REL_BUCKETS = 32
REL_MAX_DIST = 128

XA_HEADS = 4
XA_DIM = 128

PEER_HEADS = 8
PEER_KEYS = 128
PEER_DKEY = 128
PEER_TOPK = 16

LANES = 128
HEAD_PAD = LANES

Z_RQ, Z_RK, Z_RV, Z_RG = 0, 512, 1024, 1536
Z_SQ = 2048
Z_GATE = 3072
Z_QA = 6144
Z_SK = 6400
Z_SV = 6656
Z_KVA = 6912
Z_KRP = 7040
Z_KRR = 7168
Z_COLS = 7296

VMEM_LIMIT = 56 * 1024 * 1024


def _params(*sem):
    return pltpu.CompilerParams(dimension_semantics=sem, vmem_limit_bytes=VMEM_LIMIT)


def _rms(x, g=None):
    y = x * lax.rsqrt(jnp.mean(x * x, axis=-1, keepdims=True) + EPS)
    return y if g is None else y * g


def _norm_matmul_kernel(x_ref, g_ref, w_ref, o_ref, h_ref):
    @pl.when(pl.program_id(1) == 0)
    def _():
        h_ref[...] = _rms(x_ref[...].astype(F32), g_ref[...]).astype(BF16)

    o_ref[...] = jnp.dot(h_ref[...], w_ref[...], preferred_element_type=F32).astype(o_ref.dtype)


def norm_matmul(x, g, w, bm, bn, out_dtype):
    m, k = x.shape
    n = w.shape[1]
    return pl.pallas_call(
        _norm_matmul_kernel,
        grid=(m // bm, n // bn),
        in_specs=[pl.BlockSpec((bm, k), lambda i, j: (i, 0)),
                  pl.BlockSpec((1, k), lambda i, j: (0, 0)),
                  pl.BlockSpec((k, bn), lambda i, j: (0, j))],
        out_specs=pl.BlockSpec((bm, bn), lambda i, j: (i, j)),
        out_shape=jax.ShapeDtypeStruct((m, n), out_dtype),
        scratch_shapes=[pltpu.VMEM((bm, k), BF16)],
        compiler_params=_params("parallel", "arbitrary"),
        name="norm_matmul",
    )(x, g.reshape(1, k), w)


def _retention_kernel(q_ref, k_ref, v_ref, g_ref, cos_ref, sin_ref, intra_ref, qw_ref, kw_ref,
                      o_ref, state_ref, *, chunk_decay):
    @pl.when(pl.program_id(1) == 0)
    def _():
        state_ref[...] = jnp.zeros_like(state_ref)

    half = RET_DIM // 2
    c = RET_CHUNK
    for h in range(RET_HEADS):
        cols = slice(h * RET_DIM, (h + 1) * RET_DIM)
        state = state_ref[h]
        for n in range(q_ref.shape[0] // c):
            rows = slice(n * c, (n + 1) * c)
            cos = cos_ref[rows, :]
            sin = sin_ref[rows, :]
            rope = lambda t: t * cos + pltpu.roll(t, half, 1) * sin
            q = rope(q_ref[rows, cols].astype(F32)).astype(BF16)
            k = rope(k_ref[rows, cols].astype(F32)) * (RET_DIM ** -0.5)
            v = v_ref[rows, cols]
            scores = lax.dot_general(q, k.astype(BF16), (((1,), (1,)), ((), ())),
                                     preferred_element_type=F32) * intra_ref[h]
            inner = jnp.dot(scores.astype(BF16), v, preferred_element_type=F32)
            cross = jnp.dot(q, state.astype(BF16), preferred_element_type=F32) * qw_ref[h]
            kv = lax.dot_general((k * kw_ref[h]).astype(BF16), v, (((0,), (0,)), ((), ())),
                                 preferred_element_type=F32)
            state = chunk_decay[h] * state + kv
            y = _rms(inner + cross)
            gate = g_ref[rows, cols].astype(F32)
            o_ref[rows, cols] = (y * (gate * jax.nn.sigmoid(gate))).astype(o_ref.dtype)
        state_ref[h] = state


def retention(z, batch, seq, cos, sin, intra, qw, kw, chunk_decay, chunks_per_step):
    c = RET_CHUNK
    rows = c * chunks_per_step
    nc = seq // rows
    w = RET_HEADS * RET_DIM
    zspec = lambda col: pl.BlockSpec((rows, w), lambda b, n, col=col: (b * nc + n, col // w))
    tab = pl.BlockSpec((rows, RET_DIM), lambda b, n: (n, 0))
    const = pl.BlockSpec((RET_HEADS, c, c), lambda b, n: (0, 0, 0))
    return pl.pallas_call(
        functools.partial(_retention_kernel, chunk_decay=chunk_decay),
        grid=(batch, nc),
        in_specs=[zspec(Z_RQ), zspec(Z_RK), zspec(Z_RV), zspec(Z_RG), tab, tab, const, const, const],
        out_specs=pl.BlockSpec((rows, w), lambda b, n: (b * nc + n, 0)),
        out_shape=jax.ShapeDtypeStruct((batch * seq, w), BF16),
        scratch_shapes=[pltpu.VMEM((RET_HEADS, RET_DIM, RET_DIM), F32)],
        compiler_params=_params("parallel", "arbitrary"),
        name="retention",
    )(z, z, z, z, cos, sin, intra, qw, kw)


def _mla_q_kernel(qa_ref, g_ref, w1_ref, w2_ref, cos_ref, sin_ref, o_ref):
    qn = _rms(qa_ref[...].astype(F32), g_ref[...]).astype(BF16)
    a = jnp.dot(qn, w1_ref[...], preferred_element_type=F32)
    b = jnp.dot(qn, w2_ref[...], preferred_element_type=F32)
    cos = cos_ref[...]
    sin = sin_ref[...]
    scale = (MLA_NOPE + MLA_ROPE) ** -0.5 * math.log2(math.e)
    for h in range(MLA_HEADS):
        cols = slice(h * HEAD_PAD, (h + 1) * HEAD_PAD)
        o_ref[:, cols] = ((a[:, cols] * cos + b[:, cols] * sin) * scale).astype(o_ref.dtype)


def mla_q(z, g, w1, w2, cos, sin, seq, bm):
    t = z.shape[0]
    wd = MLA_HEADS * HEAD_PAD
    nsb = seq // bm
    return pl.pallas_call(
        _mla_q_kernel,
        grid=(t // bm,),
        in_specs=[pl.BlockSpec((bm, MLA_Q_RANK), lambda i: (i, Z_QA // MLA_Q_RANK)),
                  pl.BlockSpec((1, MLA_Q_RANK), lambda i: (0, 0)),
                  pl.BlockSpec((MLA_Q_RANK, wd), lambda i: (0, 0)),
                  pl.BlockSpec((MLA_Q_RANK, wd), lambda i: (0, 0)),
                  pl.BlockSpec((bm, HEAD_PAD), lambda i: (i % nsb, 0)),
                  pl.BlockSpec((bm, HEAD_PAD), lambda i: (i % nsb, 0))],
        out_specs=pl.BlockSpec((bm, wd), lambda i: (i, 0)),
        out_shape=jax.ShapeDtypeStruct((t, wd), BF16),
        compiler_params=_params("parallel"),
        name="mla_q",
    )(z, g.reshape(1, -1), w1, w2, cos, sin)


def _mla_kv_kernel(kva_ref, krp_ref, krr_ref, g_ref, wk_ref, wv_ref, cos_ref, sin_ref, k_ref, v_ref):
    kvn = _rms(kva_ref[...].astype(F32), g_ref[...]).astype(BF16)
    k = jnp.dot(kvn, wk_ref[...], preferred_element_type=F32)
    krope = krp_ref[...].astype(F32) * cos_ref[...] + krr_ref[...].astype(F32) * sin_ref[...]
    for h in range(MLA_HEADS):
        cols = slice(h * HEAD_PAD, (h + 1) * HEAD_PAD)
        k_ref[:, cols] = (k[:, cols] + krope).astype(k_ref.dtype)
    v_ref[...] = jnp.dot(kvn, wv_ref[...], preferred_element_type=F32).astype(v_ref.dtype)


def mla_kv(z, g, wk, wv, cos, sin, seq, bm):
    t = z.shape[0]
    wd = MLA_HEADS * HEAD_PAD
    nsb = seq // bm
    zs = lambda col: pl.BlockSpec((bm, LANES), lambda i, col=col: (i, col // LANES))
    tab = pl.BlockSpec((bm, HEAD_PAD), lambda i: (i % nsb, 0))
    wspec = pl.BlockSpec((MLA_KV_RANK, wd), lambda i: (0, 0))
    out = pl.BlockSpec((bm, wd), lambda i: (i, 0))
    return pl.pallas_call(
        _mla_kv_kernel,
        grid=(t // bm,),
        in_specs=[zs(Z_KVA), zs(Z_KRP), zs(Z_KRR), pl.BlockSpec((1, MLA_KV_RANK), lambda i: (0, 0)),
                  wspec, wspec, tab, tab],
        out_specs=[out, out],
        out_shape=[jax.ShapeDtypeStruct((t, wd), BF16)] * 2,
        compiler_params=_params("parallel"),
        name="mla_kv",
    )(z, z, z, g.reshape(1, -1), wk, wv, cos, sin)


def _mla_attn_kernel(q_ref, k_ref, v_ref, o_ref, *, bq, n_split, heads):
    qi = pl.program_id(2)
    rows = bq // n_split
    chains = [(h, r) for h in range(heads) for r in range(n_split)]
    cols = lambda h: slice(h * HEAD_PAD, (h + 1) * HEAD_PAD)
    qs = [q_ref[pl.ds(r * rows, rows), cols(h)] for h, r in chains]

    def step(j, carry, diagonal):
        start = pl.multiple_of(j * bq, bq)
        out = []
        for c, (h, r) in enumerate(chains):
            m, l, acc = carry[c]
            n_keys = (r + 1) * rows if diagonal else bq
            kb = k_ref[pl.ds(start, n_keys), cols(h)]
            vb = v_ref[pl.ds(start, n_keys), cols(h)]
            s = lax.dot_general(qs[c], kb, (((1,), (1,)), ((), ())), preferred_element_type=F32)
            if diagonal:
                row = r * rows + lax.broadcasted_iota(jnp.int32, (rows, n_keys), 0)
                col = lax.broadcasted_iota(jnp.int32, (rows, n_keys), 1)
                s = jnp.where(col <= row, s, NEG_INF)
            m_new = jnp.maximum(m, jnp.max(s, axis=-1, keepdims=True))
            alpha = jnp.exp2(m - m_new)
            p = jnp.exp2(s - m_new)
            l = alpha * l + jnp.sum(p, axis=-1, keepdims=True)
            acc = alpha * acc + jnp.dot(p.astype(BF16), vb, preferred_element_type=F32)
            out.append((m_new, l, acc))
        return tuple(out)

    init = tuple((jnp.full((rows, 1), NEG_INF, F32), jnp.zeros((rows, 1), F32),
                  jnp.zeros((rows, HEAD_PAD), F32)) for _ in chains)
    carry = lax.fori_loop(0, qi, functools.partial(step, diagonal=False), init)
    carry = step(qi, carry, diagonal=True)
    per_block = HEAD_PAD // MLA_V
    for r in range(n_split):
        outs = []
        for h in range(heads):
            _, l, acc = carry[chains.index((h, r))]
            outs.append((acc / l)[:, :MLA_V])
        for b in range(heads // per_block):
            o_ref[pl.ds(r * rows, rows), cols(b)] = jnp.concatenate(
                outs[b * per_block:(b + 1) * per_block], axis=1).astype(o_ref.dtype)


def mla_attention(q, k, v, batch, seq, bq, n_split=2, heads=2):
    nq = seq // bq
    width = heads * HEAD_PAD
    qspec = pl.BlockSpec((bq, width), lambda b, h, i: (b * nq + i, h))
    kvspec = pl.BlockSpec((seq, width), lambda b, h, i: (b, h))
    return pl.pallas_call(
        functools.partial(_mla_attn_kernel, bq=bq, n_split=n_split, heads=heads),
        grid=(batch, MLA_HEADS // heads, nq),
        in_specs=[qspec, kvspec, kvspec],
        out_specs=pl.BlockSpec((bq, heads * MLA_V), lambda b, h, i: (b * nq + i, h)),
        out_shape=jax.ShapeDtypeStruct((q.shape[0], MLA_HEADS * MLA_V), BF16),
        compiler_params=_params("parallel", "parallel", "arbitrary"),
        name="mla_attention",
    )(q, k, v)


def _swa_kernel(q_ref, kc_ref, kp_ref, vc_ref, vp_ref, bias_ref, sink_ref, o_ref, *, n_win):
    w = SWA_WINDOW
    grp = SWA_HEADS // SWA_KV_HEADS
    first = pl.program_id(1) == 0
    prev_half = lax.broadcasted_iota(jnp.int32, (grp * w, 2 * w), 1) < w
    for r in range(n_win):
        rows = slice(r * w, (r + 1) * w)
        for g in range(SWA_KV_HEADS):
            kcols = slice(g * HEAD_PAD, (g + 1) * HEAD_PAD)
            if r == 0:
                kb = jnp.concatenate([kp_ref[:, kcols], kc_ref[0:w, kcols]], axis=0)
                vb = jnp.concatenate([vp_ref[:, kcols], vc_ref[0:w, kcols]], axis=0)
            else:
                kb = kc_ref[(r - 1) * w:(r + 1) * w, kcols]
                vb = vc_ref[(r - 1) * w:(r + 1) * w, kcols]
            q = jnp.concatenate(
                [q_ref[rows, (g * grp + j) * HEAD_PAD:(g * grp + j + 1) * HEAD_PAD] for j in range(grp)], axis=0)
            s = lax.dot_general(q, kb, (((1,), (1,)), ((), ())), preferred_element_type=F32)
            s = s * (SWA_DIM ** -0.5) + bias_ref[g]
            if r == 0:
                s = jnp.where(jnp.logical_and(first, prev_half), NEG_INF, s)
            sink = sink_ref[g]
            m = jnp.maximum(jnp.max(s, axis=-1, keepdims=True), sink)
            e = jnp.exp(s - m)
            p = e / (jnp.sum(e, axis=-1, keepdims=True) + jnp.exp(sink - m))
            out = jnp.dot(p.astype(BF16), vb, preferred_element_type=F32)
            for j in range(grp):
                o_ref[rows, (g * grp + j) * HEAD_PAD:(g * grp + j + 1) * HEAD_PAD] = (
                    out[j * w:(j + 1) * w].astype(o_ref.dtype))


def swa(z, batch, seq, bias, sinks, n_win):
    w = SWA_WINDOW
    rows = n_win * w
    ns = seq // rows
    qw = SWA_HEADS * HEAD_PAD
    kw = SWA_KV_HEADS * HEAD_PAD
    grp = SWA_HEADS // SWA_KV_HEADS
    cur = lambda col: pl.BlockSpec((rows, kw), lambda b, n, col=col: (b * ns + n, col // kw))
    prev = lambda col: pl.BlockSpec(
        (w, kw), lambda b, n, col=col: ((b * ns + n) * n_win - jnp.minimum(n, 1), col // kw))
    return pl.pallas_call(
        functools.partial(_swa_kernel, n_win=n_win),
        grid=(batch, ns),
        in_specs=[pl.BlockSpec((rows, qw), lambda b, n: (b * ns + n, Z_SQ // qw)),
                  cur(Z_SK), prev(Z_SK), cur(Z_SV), prev(Z_SV),
                  pl.BlockSpec((SWA_KV_HEADS, grp * w, 2 * w), lambda b, n: (0, 0, 0)),
                  pl.BlockSpec((SWA_KV_HEADS, grp * w, 1), lambda b, n: (0, 0, 0))],
        out_specs=pl.BlockSpec((rows, qw), lambda b, n: (b * ns + n, 0)),
        out_shape=jax.ShapeDtypeStruct((batch * seq, qw), BF16),
        compiler_params=_params("parallel", "arbitrary"),
        name="swa",
    )(z, z, z, z, z, bias, sinks)


def _merge_xattn_kernel(x_ref, yr_ref, ym_ref, ys_ref, gl_ref, wr_ref, wm_ref, ws_ref, wo_ref,
                        g_ref, wq_ref, kv_ref, wxo_ref, o_ref):
    d = x_ref.shape[1]
    merged = None
    for i, (y_ref, w_ref) in enumerate(((yr_ref, wr_ref), (ym_ref, wm_ref), (ys_ref, ws_ref))):
        gate = jax.nn.sigmoid(gl_ref[:, i * d:(i + 1) * d].astype(F32))
        term = gate * jnp.dot(y_ref[...], w_ref[...], preferred_element_type=F32)
        merged = term if merged is None else merged + term
    x = x_ref[...] + jnp.dot(merged.astype(BF16), wo_ref[...], preferred_element_type=F32)

    hn = _rms(x, g_ref[...]).astype(BF16)
    q = (jnp.dot(hn, wq_ref[...], preferred_element_type=F32) * (XA_DIM ** -0.5)).astype(BF16)
    width = XA_HEADS * XA_DIM
    outs = []
    for h in range(XA_HEADS):
        cols = slice(h * XA_DIM, (h + 1) * XA_DIM)
        k = kv_ref[0, :, cols]
        v = kv_ref[0, :, width + h * XA_DIM:width + (h + 1) * XA_DIM]
        s = lax.dot_general(q[:, cols], k, (((1,), (1,)), ((), ())), preferred_element_type=F32)
        e = jnp.exp(s - jnp.max(s, axis=-1, keepdims=True))
        p = e / jnp.sum(e, axis=-1, keepdims=True)
        outs.append(jnp.dot(p.astype(BF16), v, preferred_element_type=F32).astype(BF16))
    o = jnp.concatenate(outs, axis=1)
    o_ref[...] = x + jnp.dot(o, wxo_ref[...], preferred_element_type=F32)


def merge_xattn(x, y_ret, y_mla, y_swa, z, wr, wm, ws, wo, g, wq, memkv, wxo, seq, bm, row0, rows):
    d = x.shape[1]
    nsb = seq // bm
    b0 = row0 // bm
    row = lambda width: pl.BlockSpec((bm, width), lambda i: (b0 + i, 0))
    full = lambda a: pl.BlockSpec(a.shape, lambda i: (0, 0))
    return pl.pallas_call(
        _merge_xattn_kernel,
        grid=(rows // bm,),
        in_specs=[row(d), row(y_ret.shape[1]), row(y_mla.shape[1]), row(y_swa.shape[1]),
                  pl.BlockSpec((bm, N_BRANCH * d), lambda i: (b0 + i, Z_GATE // (N_BRANCH * d))),
                  full(wr), full(wm), full(ws), full(wo),
                  pl.BlockSpec((1, d), lambda i: (0, 0)), full(wq),
                  pl.BlockSpec((1,) + memkv.shape[1:], lambda i: ((b0 + i) // nsb, 0, 0)), full(wxo)],
        out_specs=pl.BlockSpec((bm, d), lambda i: (i, 0)),
        out_shape=jax.ShapeDtypeStruct((rows, d), F32),
        compiler_params=_params("parallel"),
        name="merge_xattn",
    )(x, y_ret, y_mla, y_swa, z, wr, wm, ws, wo, g.reshape(1, d), wq, memkv, wxo)


_PEER_CAND = [(a, b) for a in range(PEER_TOPK) for b in range(PEER_TOPK) if (a + 1) * (b + 1) <= PEER_TOPK]
_PEER_NCAND = len(_PEER_CAND)
_PEER_CAND_ROWS = -(-_PEER_NCAND // 8) * 8


def _top16_rows(sc, vals_ref, idx_ref, lanes, payload=None):
    n = sc.shape[0]
    rows = lax.broadcasted_iota(jnp.int32, sc.shape, 0).astype(F32)
    for r in range(PEER_TOPK):
        m = jnp.max(sc, axis=0, keepdims=True)
        win = jnp.min(jnp.where(sc == m, rows, float(n)), axis=0, keepdims=True)
        hit = rows == win
        vals_ref[pl.ds(r, 1), lanes] = m
        if payload is None:
            idx_ref[pl.ds(r, 1), lanes] = win
        else:
            idx_ref[pl.ds(r, 1), lanes] = jnp.sum(jnp.where(hit, payload, 0.0), axis=0, keepdims=True)
        sc = jnp.where(hit, -jnp.inf, sc)


def _route_scratch(bm):
    tk = pltpu.VMEM((PEER_TOPK, bm), F32)
    cand = pltpu.VMEM((_PEER_CAND_ROWS, bm), F32)
    slot = pltpu.VMEM((PEER_HEADS * PEER_TOPK, bm), F32)
    return [tk, tk, tk, tk, cand, cand, tk, tk, slot, slot, slot]


def _route_scores(h, qh, keys_ref):
    return [lax.dot_general(keys_ref[2 * h + p], qh, (((1,), (1,)), ((), ())),
                            preferred_element_type=F32) for p in range(2)]


def _route_pieces(h, score, n_tokens, v1_ref, n1_ref, v2_ref, n2_ref, cs_ref, ci_ref, bs_ref, bi_ref,
                  gt_ref, i1t_ref, i2t_ref):
    k = PEER_TOPK
    out_rows = pl.ds(h * k if isinstance(h, int) else pl.multiple_of(h * k, k), k)

    def first_stage(p, lanes):
        v_ref, n_ref = ((v1_ref, n1_ref), (v2_ref, n2_ref))[p]
        _top16_rows(score(p, lanes), v_ref, n_ref, lanes)

    def second_stage(lanes):
        for c, (a, b) in enumerate(_PEER_CAND):
            cs_ref[pl.ds(c, 1), lanes] = v1_ref[pl.ds(a, 1), lanes] + v2_ref[pl.ds(b, 1), lanes]
            ci_ref[pl.ds(c, 1), lanes] = n1_ref[pl.ds(a, 1), lanes] * PEER_KEYS + n2_ref[pl.ds(b, 1), lanes]
        _top16_rows(cs_ref[:, lanes], bs_ref, bi_ref, lanes, payload=ci_ref[:, lanes])
        best = bs_ref[:, lanes]
        e = jnp.exp(best - best[0:1, :])
        gt_ref[out_rows, lanes] = e / jnp.sum(e, axis=0, keepdims=True)
        flat = bi_ref[:, lanes]
        first = jnp.floor(flat * (1.0 / PEER_KEYS))
        i1t_ref[out_rows, lanes] = first
        i2t_ref[out_rows, lanes] = flat - first * PEER_KEYS

    pieces = []
    for c in range(n_tokens // LANES):
        lanes = slice(c * LANES, (c + 1) * LANES)
        pieces += [functools.partial(first_stage, 0, lanes), functools.partial(first_stage, 1, lanes),
                   functools.partial(second_stage, lanes)]
    return pieces


def _interleave(a, b):
    i = j = 0
    while i < len(a) or j < len(b):
        if j >= len(b) or (i < len(a) and i * len(b) <= j * len(a)):
            a[i]()
            i += 1
        else:
            b[j]()
            j += 1


def _route_init(cs_ref, ci_ref):
    cs_ref[...] = jnp.full(cs_ref.shape, -jnp.inf, F32)
    ci_ref[...] = jnp.zeros(ci_ref.shape, F32)


def _peer_route_kernel(x_ref, g_ref, wq_ref, keys_ref, hn_ref, gate_ref, i1_ref, i2_ref, *scratch):
    hn = _rms(x_ref[...], g_ref[...]).astype(BF16)
    hn_ref[...] = hn
    q = jnp.dot(hn, wq_ref[...], preferred_element_type=F32).astype(BF16)
    _route_init(scratch[4], scratch[5])
    for h in range(PEER_HEADS):
        scores = _route_scores(h, q[:, h * PEER_DKEY:(h + 1) * PEER_DKEY], keys_ref)
        for piece in _route_pieces(h, lambda p, lanes: scores[p][:, lanes], x_ref.shape[0], *scratch):
            piece()
    gt_ref, i1t_ref, i2t_ref = scratch[-3:]
    gate_ref[...] = gt_ref[...].T
    i1_ref[...] = i1t_ref[...].T
    i2_ref[...] = i2t_ref[...].T


def peer_route(x, g, wq, keys, bm):
    rows, d = x.shape
    slots = PEER_HEADS * PEER_TOPK
    row = lambda width: pl.BlockSpec((bm, width), lambda i: (i, 0))
    return pl.pallas_call(
        _peer_route_kernel,
        grid=(rows // bm,),
        in_specs=[row(d), pl.BlockSpec((1, d), lambda i: (0, 0)),
                  pl.BlockSpec(wq.shape, lambda i: (0, 0)),
                  pl.BlockSpec(keys.shape, lambda i: (0, 0, 0))],
        out_specs=[row(d), row(slots), row(slots), row(slots)],
        out_shape=[jax.ShapeDtypeStruct((rows, d), BF16)] + [jax.ShapeDtypeStruct((rows, slots), F32)] * 3,
        scratch_shapes=_route_scratch(bm),
        compiler_params=_params("parallel"),
        name="peer_route",
    )(x, g.reshape(1, d), wq, keys)


G_PITCH = 132
G_TOKENS_PER_MATMUL = 16
G_COPY_GROUP = 32


def _peer_gates_kernel(gate_ref, i1_ref, i2_ref, o_ref, *s_refs):
    nk = PEER_KEYS
    slots = gate_ref.shape[2]
    sub = G_TOKENS_PER_MATMUL
    grp = G_COPY_GROUP
    key = lax.broadcasted_iota(jnp.int32, (sub, nk, slots), 1).astype(F32)

    def build(gi):
        for c in range(grp // sub):
            tok = pl.ds(gi * grp + c * sub, sub)
            wa = jnp.where(i1_ref[tok] == key, gate_ref[tok], 0.0).astype(BF16)
            wb = jnp.where(i2_ref[tok] == key, 1.0, 0.0).astype(BF16)
            g = lax.dot_general(wa, wb, (((2,), (2,)), ((0,), (0,))),
                                preferred_element_type=F32)
            for u in range(sub):
                s_refs[gi][pl.ds((c * sub + u) * G_PITCH, nk), :] = g[u]

    def gather(gi):
        for i in range(nk):
            o_ref[gi * grp:(gi + 1) * grp, i * nk:(i + 1) * nk] = (
                s_refs[gi][pl.ds(i, grp, stride=G_PITCH), :].astype(o_ref.dtype))

    for gi in range(len(s_refs)):
        build(gi)
        if gi:
            gather(gi - 1)
    gather(len(s_refs) - 1)


def peer_gates(gate, i1, i2, bm):
    t, slots = gate.shape
    n_exp = PEER_KEYS * PEER_KEYS
    spec = pl.BlockSpec((bm, 1, slots), lambda i: (i, 0, 0))
    r3 = lambda a: a.reshape(t, 1, slots)
    return pl.pallas_call(
        _peer_gates_kernel,
        grid=(t // bm,),
        in_specs=[spec, spec, spec],
        out_specs=pl.BlockSpec((bm, n_exp), lambda i: (i, 0)),
        out_shape=jax.ShapeDtypeStruct((t, n_exp), BF16),
        scratch_shapes=[pltpu.VMEM((G_COPY_GROUP * G_PITCH, PEER_KEYS), F32)] * (bm // G_COPY_GROUP),
        compiler_params=_params("parallel"),
        name="peer_gates",
    )(r3(gate), r3(i1), r3(i2))


EXPERT_CHUNKS = 8


def _experts_step(hn_ref, gm_ref, ut_ref, v_ref, o_ref):
    a = jnp.dot(hn_ref[...], ut_ref[...], preferred_element_type=F32)
    act = 0.5 * a * (1.0 + lax.erf(a * (2.0 ** -0.5)))
    p = (act * gm_ref[...].astype(F32)).astype(BF16)
    o_ref[...] += jnp.dot(p, v_ref[...], preferred_element_type=F32)


def _peer_expert_kernel(x_ref, hn_ref, gm_ref, ut_ref, v_ref, o_ref):
    @pl.when(pl.program_id(1) == 0)
    def _():
        o_ref[...] = x_ref[...]

    _experts_step(hn_ref, gm_ref, ut_ref, v_ref, o_ref)


def _peer_fused_kernel(x_ref, hn_ref, gm_ref, ut_ref, v_ref, xr_ref, g_ref, wq_ref, keys_ref,
                       o_ref, hnr_ref, gate_ref, i1_ref, i2_ref, s_ref, *scratch):
    j = pl.program_id(1)

    @pl.when(j == 0)
    def _():
        o_ref[...] = x_ref[...]
        hn = _rms(xr_ref[...], g_ref[...]).astype(BF16)
        hnr_ref[...] = hn
        q = jnp.dot(hn, wq_ref[...], preferred_element_type=F32).astype(BF16)
        for h in range(PEER_HEADS):
            for p, s in enumerate(_route_scores(h, q[:, h * PEER_DKEY:(h + 1) * PEER_DKEY], keys_ref)):
                s_ref[2 * h + p] = s
        _route_init(scratch[4], scratch[5])

    route = _route_pieces(j, lambda p, lanes: s_ref[2 * j + p, :, lanes], xr_ref.shape[0], *scratch)

    def expert_chunk(c):
        width = ut_ref.shape[1] // EXPERT_CHUNKS
        cols = slice(c * width, (c + 1) * width)
        a = jnp.dot(hn_ref[...], ut_ref[:, cols], preferred_element_type=F32)
        act = 0.5 * a * (1.0 + lax.erf(a * (2.0 ** -0.5)))
        p = (act * gm_ref[:, cols].astype(F32)).astype(BF16)
        o_ref[...] += jnp.dot(p, v_ref[cols, :], preferred_element_type=F32)

    _interleave([functools.partial(expert_chunk, c) for c in range(EXPERT_CHUNKS)], route)

    @pl.when(j == PEER_HEADS - 1)
    def _():
        gt_ref, i1t_ref, i2t_ref = scratch[-3:]
        gate_ref[...] = gt_ref[...].T
        i1_ref[...] = i1t_ref[...].T
        i2_ref[...] = i2t_ref[...].T


def peer_experts(x, hn, gm, ut, v, bm, bn, route=None):
    rows, d = x.shape
    n_exp = ut.shape[1]
    blk = lambda width: pl.BlockSpec((bm, width), lambda i, j: (i, 0))
    in_specs = [blk(d), blk(d),
                pl.BlockSpec((bm, bn), lambda i, j: (i, j)),
                pl.BlockSpec((d, bn), lambda i, j: (0, j)),
                pl.BlockSpec((bn, d), lambda i, j: (j, 0))]
    out_specs = [blk(d)]
    out_shape = [jax.ShapeDtypeStruct((rows, d), F32)]
    args = [x, hn, gm, ut, v]
    kern, scratch, name = _peer_expert_kernel, [], "peer_experts"
    if route is not None:
        g, wq, keys, x_next = route
        assert n_exp // bn == PEER_HEADS and x_next.shape == x.shape
        slots = PEER_HEADS * PEER_TOPK
        in_specs += [blk(d), pl.BlockSpec((1, d), lambda i, j: (0, 0)),
                     pl.BlockSpec(wq.shape, lambda i, j: (0, 0)),
                     pl.BlockSpec(keys.shape, lambda i, j: (0, 0, 0))]
        out_specs += [blk(d), blk(slots), blk(slots), blk(slots)]
        out_shape += [jax.ShapeDtypeStruct((rows, d), BF16)] + [jax.ShapeDtypeStruct((rows, slots), F32)] * 3
        args += [x_next, g.reshape(1, d), wq, keys]
        scratch = [pltpu.VMEM((2 * PEER_HEADS, PEER_KEYS, bm), F32)] + _route_scratch(bm)
        kern, name = _peer_fused_kernel, "peer_experts_route"
    out = pl.pallas_call(
        kern,
        grid=(rows // bm, n_exp // bn),
        in_specs=in_specs,
        out_specs=out_specs,
        out_shape=out_shape,
        scratch_shapes=scratch,
        compiler_params=_params("parallel", "arbitrary"),
        name=name,
    )(*args)
    return out[0] if route is None else out


def _final_norm_kernel(x_ref, g_ref, o_ref):
    o_ref[...] = _rms(x_ref[...], g_ref[...])


def final_norm(x, g, bm):
    t, d = x.shape
    return pl.pallas_call(
        _final_norm_kernel,
        grid=(t // bm,),
        in_specs=[pl.BlockSpec((bm, d), lambda i: (i, 0)), pl.BlockSpec((1, d), lambda i: (0, 0))],
        out_specs=pl.BlockSpec((bm, d), lambda i: (i, 0)),
        out_shape=jax.ShapeDtypeStruct((t, d), F32),
        compiler_params=_params("parallel"),
        name="final_norm",
    )(x, g.reshape(1, d))


def _pad_heads(w, heads, dim, axis):
    shape = w.shape[:axis] + (heads, dim) + w.shape[axis + 1:]
    w = w.reshape(shape)
    pad = [(0, 0)] * w.ndim
    pad[axis + 1] = (0, HEAD_PAD - dim)
    w = jnp.pad(w, pad)
    return w.reshape(w.shape[:axis] + (heads * HEAD_PAD,) + w.shape[axis + 2:])


def _rot_half_cols(w):
    half = w.shape[-1] // 2
    return jnp.concatenate([-w[..., half:], w[..., :half]], axis=-1)


def _pack_w_in(w_in):
    rq, rk, rv, rg, qa, kva, kr, sq, sk, sv, gate = jnp.split(
        w_in, [int(p) for p in np.cumsum(
            [512, 512, 512, 512, MLA_Q_RANK, MLA_KV_RANK, MLA_ROPE,
             SWA_HEADS * SWA_DIM, SWA_KV_HEADS * SWA_DIM, SWA_KV_HEADS * SWA_DIM])], axis=-1)
    place = lambda w: jnp.pad(w, ((0, 0), (0, 0), (MLA_NOPE, HEAD_PAD - MLA_NOPE - MLA_ROPE)))
    cols = [rq, rk, rv, rg, _pad_heads(sq, SWA_HEADS, SWA_DIM, 2), gate, qa,
            _pad_heads(sk, SWA_KV_HEADS, SWA_DIM, 2), _pad_heads(sv, SWA_KV_HEADS, SWA_DIM, 2),
            kva, place(kr), place(_rot_half_cols(kr))]
    out = jnp.concatenate(cols, axis=-1).astype(BF16)
    assert out.shape[-1] == Z_COLS
    return out


def _rope_tables(seq, half):
    inv_freq = jnp.power(ROPE_BASE, -jnp.arange(half, dtype=F32) / half)
    ang = jnp.arange(seq, dtype=F32)[:, None] * inv_freq[None, :]
    return jnp.cos(ang), jnp.sin(ang)


def _t5_buckets(rel):
    n = np.maximum(rel, 0)
    max_exact = REL_BUCKETS // 2
    large = max_exact + (np.log(np.maximum(n, 1) / max_exact) / np.log(REL_MAX_DIST / max_exact)
                         * (REL_BUCKETS - max_exact)).astype(np.int32)
    large = np.minimum(large, REL_BUCKETS - 1)
    return np.where(n < max_exact, n, large).astype(np.int32)


def _swa_bias(rel_bias):
    w = SWA_WINDOW
    rel = (np.arange(w)[:, None] + w) - np.arange(2 * w)[None, :]
    bucket = _t5_buckets(rel)
    table = rel_bias.astype(F32).T
    bias = sum(jnp.where(bucket[None] == b, table[:, b][:, None, None], 0.0) for b in range(REL_BUCKETS))
    bias = jnp.where(((rel >= 0) & (rel < w))[None], bias, NEG_INF)
    return bias.reshape(SWA_KV_HEADS, (SWA_HEADS // SWA_KV_HEADS) * w, 2 * w)


def _retention_tables():
    c = RET_CHUNK
    log_gamma = np.log(1.0 - np.exp2(-5.0 - np.arange(RET_HEADS, dtype=np.float64)))
    idx = np.arange(c, dtype=np.float64)
    diff = idx[:, None] - idx[None, :]
    intra = np.where(diff >= 0, np.exp(log_gamma[:, None, None] * np.maximum(diff, 0.0)), 0.0)
    qw = np.exp(log_gamma[:, None] * (idx + 1.0)[None, :])
    kw = np.exp(log_gamma[:, None] * (c - 1.0 - idx)[None, :])
    bc = lambda a: jnp.asarray(np.broadcast_to(a[:, :, None], (RET_HEADS, c, c)), F32)
    chunk_decay = tuple(float(v) for v in np.exp(log_gamma * c))
    return jnp.asarray(intra, F32), bc(qw), bc(kw), chunk_decay


def kernel(x, mem, rel_bias, g_mix, w_in, mla_q_norm, w_mla_qb, mla_kv_norm, w_mla_kvb, swa_sinks,
           w_branch_ret, w_branch_mla, w_branch_swa, w_out, g_xattn, g_mem, w_xq, w_xkv, w_xo,
           g_ffn, w_peer_query, peer_sub_keys, peer_u, peer_v, g_final):
    batch, seq, d = x.shape
    depth = w_in.shape[0]
    t = batch * seq
    mem_len = mem.shape[1]
    bm = min(512, seq)
    big = min(1024, seq)
    n_groups = 4 if t % (4 * bm) == 0 else 1
    group = t // n_groups
    peer_bn = PEER_KEYS * PEER_KEYS // PEER_HEADS

    w_in_p = _pack_w_in(w_in)
    qb = w_mla_qb.reshape(depth, MLA_Q_RANK, MLA_HEADS, MLA_NOPE + MLA_ROPE)
    q_nope, q_rope = qb[..., :MLA_NOPE], qb[..., MLA_NOPE:]
    zpad = jnp.zeros(qb.shape[:3] + (HEAD_PAD - MLA_NOPE - MLA_ROPE,), F32)
    flat = lambda w: w.reshape(w.shape[:2] + (MLA_HEADS * HEAD_PAD,)).astype(BF16)
    w_q1 = flat(jnp.concatenate([q_nope, q_rope, zpad], axis=-1))
    w_q2 = flat(jnp.concatenate([jnp.zeros_like(q_nope), _rot_half_cols(q_rope), zpad], axis=-1))
    kvb = w_mla_kvb.reshape(depth, MLA_KV_RANK, MLA_HEADS, MLA_NOPE + MLA_V)
    w_k = flat(jnp.pad(kvb[..., :MLA_NOPE], ((0, 0),) * 3 + ((0, HEAD_PAD - MLA_NOPE),)))
    w_v = flat(jnp.pad(kvb[..., MLA_NOPE:], ((0, 0),) * 3 + ((0, HEAD_PAD - MLA_V),)))
    w_bret = w_branch_ret.astype(BF16)
    w_bmla = w_branch_mla.astype(BF16)
    w_bswa = _pad_heads(w_branch_swa, SWA_HEADS, SWA_DIM, 1).astype(BF16)
    w_out_b = w_out.astype(BF16)
    w_xq_b, w_xkv_b, w_xo_b = w_xq.astype(BF16), w_xkv.astype(BF16), w_xo.astype(BF16)
    w_pq = w_peer_query.astype(BF16)
    half = PEER_DKEY // 2
    keys = jnp.stack([jnp.pad(peer_sub_keys[:, :, 0], ((0, 0),) * 3 + ((0, half),)),
                      jnp.pad(peer_sub_keys[:, :, 1], ((0, 0),) * 3 + ((half, 0),))], axis=2)
    keys = keys.reshape(depth, 2 * PEER_HEADS, PEER_KEYS, PEER_DKEY).astype(BF16)
    peer_ut = jnp.swapaxes(peer_u, 1, 2).astype(BF16)
    peer_vb = peer_v.astype(BF16)

    cos64, sin64 = _rope_tables(seq, RET_DIM // 2)
    ret_cos = jnp.concatenate([cos64, cos64], axis=1)
    ret_sin = jnp.concatenate([-sin64, sin64], axis=1)
    cos16, sin16 = _rope_tables(seq, MLA_ROPE // 2)
    tail = jnp.zeros((seq, HEAD_PAD - MLA_NOPE - MLA_ROPE), F32)
    mla_cos = jnp.concatenate([jnp.ones((seq, MLA_NOPE), F32), cos16, cos16, tail], axis=1)
    mla_sin = jnp.concatenate([jnp.zeros((seq, MLA_NOPE), F32), sin16, sin16, tail], axis=1)
    intra, ret_qw, ret_kw, chunk_decay = _retention_tables()
    swa_bias = _swa_bias(rel_bias)
    grp = SWA_HEADS // SWA_KV_HEADS

    x = x.reshape(t, d)
    mem2 = mem.reshape(batch * mem_len, d)
    for l in range(depth):
        z = norm_matmul(x, g_mix[l], w_in_p[l], bm=big, bn=Z_COLS // 3, out_dtype=BF16)
        y_ret = retention(z, batch, seq, ret_cos, ret_sin, intra, ret_qw, ret_kw, chunk_decay,
                          chunks_per_step=min(4, seq // RET_CHUNK))
        q = mla_q(z, mla_q_norm[l], w_q1[l], w_q2[l], mla_cos, mla_sin, seq, bm)
        k, v = mla_kv(z, mla_kv_norm[l], w_k[l], w_v[l], mla_cos, mla_sin, seq, bm)
        y_mla = mla_attention(q, k, v, batch, seq, bq=big)
        sinks = jnp.broadcast_to(swa_sinks[l].reshape(SWA_KV_HEADS, grp, 1, 1),
                                 (SWA_KV_HEADS, grp, SWA_WINDOW, 1)).reshape(SWA_KV_HEADS, grp * SWA_WINDOW, 1)
        y_swa = swa(z, batch, seq, swa_bias, sinks, n_win=min(4, seq // SWA_WINDOW))
        memkv = norm_matmul(mem2, g_mem[l], w_xkv_b[l], bm=min(512, batch * mem_len),
                            bn=w_xkv_b.shape[2], out_dtype=BF16).reshape(batch, mem_len, -1)
        xs = [merge_xattn(x, y_ret, y_mla, y_swa, z, w_bret[l], w_bmla[l], w_bswa[l], w_out_b[l],
                          g_xattn[l], w_xq_b[l], memkv, w_xo_b[l], seq, bm, row0=k * group, rows=group)
              for k in range(n_groups)]
        hn, gate, i1, i2 = peer_route(xs[0], g_ffn[l], w_pq[l], keys[l], bm=min(256, seq))
        for k in range(n_groups):
            gm = peer_gates(gate, i1, i2, bm=min(256, seq))
            if k + 1 < n_groups:
                xs[k], hn, gate, i1, i2 = peer_experts(
                    xs[k], hn, gm, peer_ut[l], peer_vb[l], bm=bm, bn=peer_bn,
                    route=(g_ffn[l], w_pq[l], keys[l], xs[k + 1]))
            else:
                xs[k] = peer_experts(xs[k], hn, gm, peer_ut[l], peer_vb[l], bm=min(big, group), bn=peer_bn)
        x = jnp.concatenate(xs, axis=0)
    return final_norm(x, g_final, bm).reshape(batch, seq, d)
```

```python
import functools
import math

import numpy as np
import jax
import jax.numpy as jnp
from jax import lax
from jax.experimental import pallas as pl
from jax.experimental.pallas import tpu as pltpu

F32 = jnp.float32
BF16 = jnp.bfloat16

EPS = 1e-6
NEG_INF = -1e30
ROPE_BASE = 10000.0
N_BRANCH = 3

RET_HEADS = 4
RET_DIM = 128
RET_CHUNK = 128

MLA_HEADS = 8
MLA_NOPE = 64
MLA_ROPE = 32
MLA_V = 64
MLA_Q_RANK = 256
MLA_KV_RANK = 128

SWA_HEADS = 8
SWA_KV_HEADS = 2
SWA_DIM = 64
SWA_WINDOW = 128
REL_BUCKETS = 32
REL_MAX_DIST = 128

XA_HEADS = 4
XA_DIM = 128

PEER_HEADS = 8
PEER_KEYS = 128
PEER_DKEY = 128
PEER_TOPK = 16

LANES = 128
HEAD_PAD = LANES

Z_RQ, Z_RK, Z_RV, Z_RG = 0, 512, 1024, 1536
Z_SQ = 2048
Z_GATE = 3072
Z_QA = 6144
Z_SK = 6400
Z_SV = 6656
Z_KVA = 6912
Z_KRP = 7040
Z_KRR = 7168
Z_COLS = 7296

VMEM_LIMIT = 56 * 1024 * 1024


def _params(*sem):
    return pltpu.CompilerParams(dimension_semantics=sem, vmem_limit_bytes=VMEM_LIMIT)


def _rms(x, g=None):
    y = x * lax.rsqrt(jnp.mean(x * x, axis=-1, keepdims=True) + EPS)
    return y if g is None else y * g


def _norm_matmul_kernel(x_ref, g_ref, w_ref, o_ref, h_ref):
    @pl.when(pl.program_id(1) == 0)
    def _():
        h_ref[...] = _rms(x_ref[...].astype(F32), g_ref[...]).astype(BF16)

    o_ref[...] = jnp.dot(h_ref[...], w_ref[...], preferred_element_type=F32).astype(o_ref.dtype)


def norm_matmul(x, g, w, bm, bn, out_dtype):
    m, k = x.shape
    n = w.shape[1]
    return pl.pallas_call(
        _norm_matmul_kernel,
        grid=(m // bm, n // bn),
        in_specs=[pl.BlockSpec((bm, k), lambda i, j: (i, 0)),
                  pl.BlockSpec((1, k), lambda i, j: (0, 0)),
                  pl.BlockSpec((k, bn), lambda i, j: (0, j))],
        out_specs=pl.BlockSpec((bm, bn), lambda i, j: (i, j)),
        out_shape=jax.ShapeDtypeStruct((m, n), out_dtype),
        scratch_shapes=[pltpu.VMEM((bm, k), BF16)],
        compiler_params=_params("parallel", "arbitrary"),
        name="norm_matmul",
    )(x, g.reshape(1, k), w)


def _retention_kernel(q_ref, k_ref, v_ref, g_ref, cos_ref, sin_ref, intra_ref, qw_ref, kw_ref,
                      o_ref, state_ref, *, chunk_decay):
    @pl.when(pl.program_id(1) == 0)
    def _():
        state_ref[...] = jnp.zeros_like(state_ref)

    half = RET_DIM // 2
    c = RET_CHUNK
    for h in range(RET_HEADS):
        cols = slice(h * RET_DIM, (h + 1) * RET_DIM)
        state = state_ref[h]
        for n in range(q_ref.shape[0] // c):
            rows = slice(n * c, (n + 1) * c)
            cos = cos_ref[rows, :]
            sin = sin_ref[rows, :]
            rope = lambda t: t * cos + pltpu.roll(t, half, 1) * sin
            q = rope(q_ref[rows, cols].astype(F32)).astype(BF16)
            k = rope(k_ref[rows, cols].astype(F32)) * (RET_DIM ** -0.5)
            v = v_ref[rows, cols]
            scores = lax.dot_general(q, k.astype(BF16), (((1,), (1,)), ((), ())),
                                     preferred_element_type=F32) * intra_ref[h]
            inner = jnp.dot(scores.astype(BF16), v, preferred_element_type=F32)
            cross = jnp.dot(q, state.astype(BF16), preferred_element_type=F32) * qw_ref[h]
            kv = lax.dot_general((k * kw_ref[h]).astype(BF16), v, (((0,), (0,)), ((), ())),
                                 preferred_element_type=F32)
            state = chunk_decay[h] * state + kv
            y = _rms(inner + cross)
            gate = g_ref[rows, cols].astype(F32)
            o_ref[rows, cols] = (y * (gate * jax.nn.sigmoid(gate))).astype(o_ref.dtype)
        state_ref[h] = state


def retention(z, batch, seq, cos, sin, intra, qw, kw, chunk_decay, chunks_per_step):
    c = RET_CHUNK
    rows = c * chunks_per_step
    nc = seq // rows
    w = RET_HEADS * RET_DIM
    zspec = lambda col: pl.BlockSpec((rows, w), lambda b, n, col=col: (b * nc + n, col // w))
    tab = pl.BlockSpec((rows, RET_DIM), lambda b, n: (n, 0))
    const = pl.BlockSpec((RET_HEADS, c, c), lambda b, n: (0, 0, 0))
    return pl.pallas_call(
        functools.partial(_retention_kernel, chunk_decay=chunk_decay),
        grid=(batch, nc),
        in_specs=[zspec(Z_RQ), zspec(Z_RK), zspec(Z_RV), zspec(Z_RG), tab, tab, const, const, const],
        out_specs=pl.BlockSpec((rows, w), lambda b, n: (b * nc + n, 0)),
        out_shape=jax.ShapeDtypeStruct((batch * seq, w), BF16),
        scratch_shapes=[pltpu.VMEM((RET_HEADS, RET_DIM, RET_DIM), F32)],
        compiler_params=_params("parallel", "arbitrary"),
        name="retention",
    )(z, z, z, z, cos, sin, intra, qw, kw)


def _mla_q_kernel(qa_ref, g_ref, w1_ref, w2_ref, cos_ref, sin_ref, o_ref):
    qn = _rms(qa_ref[...].astype(F32), g_ref[...]).astype(BF16)
    a = jnp.dot(qn, w1_ref[...], preferred_element_type=F32)
    b = jnp.dot(qn, w2_ref[...], preferred_element_type=F32)
    cos = cos_ref[...]
    sin = sin_ref[...]
    scale = (MLA_NOPE + MLA_ROPE) ** -0.5 * math.log2(math.e)
    for h in range(MLA_HEADS):
        cols = slice(h * HEAD_PAD, (h + 1) * HEAD_PAD)
        o_ref[:, cols] = ((a[:, cols] * cos + b[:, cols] * sin) * scale).astype(o_ref.dtype)


def mla_q(z, g, w1, w2, cos, sin, seq, bm):
    t = z.shape[0]
    wd = MLA_HEADS * HEAD_PAD
    nsb = seq // bm
    return pl.pallas_call(
        _mla_q_kernel,
        grid=(t // bm,),
        in_specs=[pl.BlockSpec((bm, MLA_Q_RANK), lambda i: (i, Z_QA // MLA_Q_RANK)),
                  pl.BlockSpec((1, MLA_Q_RANK), lambda i: (0, 0)),
                  pl.BlockSpec((MLA_Q_RANK, wd), lambda i: (0, 0)),
                  pl.BlockSpec((MLA_Q_RANK, wd), lambda i: (0, 0)),
                  pl.BlockSpec((bm, HEAD_PAD), lambda i: (i % nsb, 0)),
                  pl.BlockSpec((bm, HEAD_PAD), lambda i: (i % nsb, 0))],
        out_specs=pl.BlockSpec((bm, wd), lambda i: (i, 0)),
        out_shape=jax.ShapeDtypeStruct((t, wd), BF16),
        compiler_params=_params("parallel"),
        name="mla_q",
    )(z, g.reshape(1, -1), w1, w2, cos, sin)


def _mla_kv_kernel(kva_ref, krp_ref, krr_ref, g_ref, wk_ref, wv_ref, cos_ref, sin_ref, k_ref, v_ref):
    kvn = _rms(kva_ref[...].astype(F32), g_ref[...]).astype(BF16)
    k = jnp.dot(kvn, wk_ref[...], preferred_element_type=F32)
    krope = krp_ref[...].astype(F32) * cos_ref[...] + krr_ref[...].astype(F32) * sin_ref[...]
    for h in range(MLA_HEADS):
        cols = slice(h * HEAD_PAD, (h + 1) * HEAD_PAD)
        k_ref[:, cols] = (k[:, cols] + krope).astype(k_ref.dtype)
    v_ref[...] = jnp.dot(kvn, wv_ref[...], preferred_element_type=F32).astype(v_ref.dtype)


def mla_kv(z, g, wk, wv, cos, sin, seq, bm):
    t = z.shape[0]
    wd = MLA_HEADS * HEAD_PAD
    nsb = seq // bm
    zs = lambda col: pl.BlockSpec((bm, LANES), lambda i, col=col: (i, col // LANES))
    tab = pl.BlockSpec((bm, HEAD_PAD), lambda i: (i % nsb, 0))
    wspec = pl.BlockSpec((MLA_KV_RANK, wd), lambda i: (0, 0))
    out = pl.BlockSpec((bm, wd), lambda i: (i, 0))
    return pl.pallas_call(
        _mla_kv_kernel,
        grid=(t // bm,),
        in_specs=[zs(Z_KVA), zs(Z_KRP), zs(Z_KRR), pl.BlockSpec((1, MLA_KV_RANK), lambda i: (0, 0)),
                  wspec, wspec, tab, tab],
        out_specs=[out, out],
        out_shape=[jax.ShapeDtypeStruct((t, wd), BF16)] * 2,
        compiler_params=_params("parallel"),
        name="mla_kv",
    )(z, z, z, g.reshape(1, -1), wk, wv, cos, sin)


def _mla_attn_kernel(q_ref, k_ref, v_ref, o_ref, *, bq, n_split, heads):
    qi = pl.program_id(2)
    rows = bq // n_split
    chains = [(h, r) for h in range(heads) for r in range(n_split)]
    cols = lambda h: slice(h * HEAD_PAD, (h + 1) * HEAD_PAD)
    qs = [q_ref[pl.ds(r * rows, rows), cols(h)] for h, r in chains]

    def step(j, carry, diagonal):
        start = pl.multiple_of(j * bq, bq)
        out = []
        for c, (h, r) in enumerate(chains):
            m, l, acc = carry[c]
            n_keys = (r + 1) * rows if diagonal else bq
            kb = k_ref[pl.ds(start, n_keys), cols(h)]
            vb = v_ref[pl.ds(start, n_keys), cols(h)]
            s = lax.dot_general(qs[c], kb, (((1,), (1,)), ((), ())), preferred_element_type=F32)
            if diagonal:
                row = r * rows + lax.broadcasted_iota(jnp.int32, (rows, n_keys), 0)
                col = lax.broadcasted_iota(jnp.int32, (rows, n_keys), 1)
                s = jnp.where(col <= row, s, NEG_INF)
            m_new = jnp.maximum(m, jnp.max(s, axis=-1, keepdims=True))
            alpha = jnp.exp2(m - m_new)
            p = jnp.exp2(s - m_new)
            l = alpha * l + jnp.sum(p, axis=-1, keepdims=True)
            acc = alpha * acc + jnp.dot(p.astype(BF16), vb, preferred_element_type=F32)
            out.append((m_new, l, acc))
        return tuple(out)

    init = tuple((jnp.full((rows, 1), NEG_INF, F32), jnp.zeros((rows, 1), F32),
                  jnp.zeros((rows, HEAD_PAD), F32)) for _ in chains)
    carry = lax.fori_loop(0, qi, functools.partial(step, diagonal=False), init)
    carry = step(qi, carry, diagonal=True)
    per_block = HEAD_PAD // MLA_V
    for r in range(n_split):
        outs = []
        for h in range(heads):
            _, l, acc = carry[chains.index((h, r))]
            outs.append((acc / l)[:, :MLA_V])
        for b in range(heads // per_block):
            o_ref[pl.ds(r * rows, rows), cols(b)] = jnp.concatenate(
                outs[b * per_block:(b + 1) * per_block], axis=1).astype(o_ref.dtype)


def mla_attention(q, k, v, batch, seq, bq, n_split=2, heads=2):
    nq = seq // bq
    width = heads * HEAD_PAD
    qspec = pl.BlockSpec((bq, width), lambda b, h, i: (b * nq + i, h))
    kvspec = pl.BlockSpec((seq, width), lambda b, h, i: (b, h))
    return pl.pallas_call(
        functools.partial(_mla_attn_kernel, bq=bq, n_split=n_split, heads=heads),
        grid=(batch, MLA_HEADS // heads, nq),
        in_specs=[qspec, kvspec, kvspec],
        out_specs=pl.BlockSpec((bq, heads * MLA_V), lambda b, h, i: (b * nq + i, h)),
        out_shape=jax.ShapeDtypeStruct((q.shape[0], MLA_HEADS * MLA_V), BF16),
        compiler_params=_params("parallel", "parallel", "arbitrary"),
        name="mla_attention",
    )(q, k, v)


def _swa_kernel(q_ref, kc_ref, kp_ref, vc_ref, vp_ref, bias_ref, sink_ref, o_ref, *, n_win):
    w = SWA_WINDOW
    grp = SWA_HEADS // SWA_KV_HEADS
    first = pl.program_id(1) == 0
    prev_half = lax.broadcasted_iota(jnp.int32, (grp * w, 2 * w), 1) < w
    for r in range(n_win):
        rows = slice(r * w, (r + 1) * w)
        for g in range(SWA_KV_HEADS):
            kcols = slice(g * HEAD_PAD, (g + 1) * HEAD_PAD)
            if r == 0:
                kb = jnp.concatenate([kp_ref[:, kcols], kc_ref[0:w, kcols]], axis=0)
                vb = jnp.concatenate([vp_ref[:, kcols], vc_ref[0:w, kcols]], axis=0)
            else:
                kb = kc_ref[(r - 1) * w:(r + 1) * w, kcols]
                vb = vc_ref[(r - 1) * w:(r + 1) * w, kcols]
            q = jnp.concatenate(
                [q_ref[rows, (g * grp + j) * HEAD_PAD:(g * grp + j + 1) * HEAD_PAD] for j in range(grp)], axis=0)
            s = lax.dot_general(q, kb, (((1,), (1,)), ((), ())), preferred_element_type=F32)
            s = s * (SWA_DIM ** -0.5) + bias_ref[g]
            if r == 0:
                s = jnp.where(jnp.logical_and(first, prev_half), NEG_INF, s)
            sink = sink_ref[g]
            m = jnp.maximum(jnp.max(s, axis=-1, keepdims=True), sink)
            e = jnp.exp(s - m)
            p = e / (jnp.sum(e, axis=-1, keepdims=True) + jnp.exp(sink - m))
            out = jnp.dot(p.astype(BF16), vb, preferred_element_type=F32)
            for j in range(grp):
                o_ref[rows, (g * grp + j) * HEAD_PAD:(g * grp + j + 1) * HEAD_PAD] = (
                    out[j * w:(j + 1) * w].astype(o_ref.dtype))


def swa(z, batch, seq, bias, sinks, n_win):
    w = SWA_WINDOW
    rows = n_win * w
    ns = seq // rows
    qw = SWA_HEADS * HEAD_PAD
    kw = SWA_KV_HEADS * HEAD_PAD
    grp = SWA_HEADS // SWA_KV_HEADS
    cur = lambda col: pl.BlockSpec((rows, kw), lambda b, n, col=col: (b * ns + n, col // kw))
    prev = lambda col: pl.BlockSpec(
        (w, kw), lambda b, n, col=col: ((b * ns + n) * n_win - jnp.minimum(n, 1), col // kw))
    return pl.pallas_call(
        functools.partial(_swa_kernel, n_win=n_win),
        grid=(batch, ns),
        in_specs=[pl.BlockSpec((rows, qw), lambda b, n: (b * ns + n, Z_SQ // qw)),
                  cur(Z_SK), prev(Z_SK), cur(Z_SV), prev(Z_SV),
                  pl.BlockSpec((SWA_KV_HEADS, grp * w, 2 * w), lambda b, n: (0, 0, 0)),
                  pl.BlockSpec((SWA_KV_HEADS, grp * w, 1), lambda b, n: (0, 0, 0))],
        out_specs=pl.BlockSpec((rows, qw), lambda b, n: (b * ns + n, 0)),
        out_shape=jax.ShapeDtypeStruct((batch * seq, qw), BF16),
        compiler_params=_params("parallel", "arbitrary"),
        name="swa",
    )(z, z, z, z, z, bias, sinks)


def _merge_xattn_kernel(x_ref, yr_ref, ym_ref, ys_ref, gl_ref, wr_ref, wm_ref, ws_ref, wo_ref,
                        g_ref, wq_ref, kv_ref, wxo_ref, o_ref):
    d = x_ref.shape[1]
    merged = None
    for i, (y_ref, w_ref) in enumerate(((yr_ref, wr_ref), (ym_ref, wm_ref), (ys_ref, ws_ref))):
        gate = jax.nn.sigmoid(gl_ref[:, i * d:(i + 1) * d].astype(F32))
        term = gate * jnp.dot(y_ref[...], w_ref[...], preferred_element_type=F32)
        merged = term if merged is None else merged + term
    x = x_ref[...] + jnp.dot(merged.astype(BF16), wo_ref[...], preferred_element_type=F32)

    hn = _rms(x, g_ref[...]).astype(BF16)
    q = (jnp.dot(hn, wq_ref[...], preferred_element_type=F32) * (XA_DIM ** -0.5)).astype(BF16)
    width = XA_HEADS * XA_DIM
    outs = []
    for h in range(XA_HEADS):
        cols = slice(h * XA_DIM, (h + 1) * XA_DIM)
        k = kv_ref[0, :, cols]
        v = kv_ref[0, :, width + h * XA_DIM:width + (h + 1) * XA_DIM]
        s = lax.dot_general(q[:, cols], k, (((1,), (1,)), ((), ())), preferred_element_type=F32)
        e = jnp.exp(s - jnp.max(s, axis=-1, keepdims=True))
        p = e / jnp.sum(e, axis=-1, keepdims=True)
        outs.append(jnp.dot(p.astype(BF16), v, preferred_element_type=F32).astype(BF16))
    o = jnp.concatenate(outs, axis=1)
    o_ref[...] = x + jnp.dot(o, wxo_ref[...], preferred_element_type=F32)


def merge_xattn(x, y_ret, y_mla, y_swa, z, wr, wm, ws, wo, g, wq, memkv, wxo, seq, bm, row0, rows):
    d = x.shape[1]
    nsb = seq // bm
    b0 = row0 // bm
    row = lambda width: pl.BlockSpec((bm, width), lambda i: (b0 + i, 0))
    full = lambda a: pl.BlockSpec(a.shape, lambda i: (0, 0))
    return pl.pallas_call(
        _merge_xattn_kernel,
        grid=(rows // bm,),
        in_specs=[row(d), row(y_ret.shape[1]), row(y_mla.shape[1]), row(y_swa.shape[1]),
                  pl.BlockSpec((bm, N_BRANCH * d), lambda i: (b0 + i, Z_GATE // (N_BRANCH * d))),
                  full(wr), full(wm), full(ws), full(wo),
                  pl.BlockSpec((1, d), lambda i: (0, 0)), full(wq),
                  pl.BlockSpec((1,) + memkv.shape[1:], lambda i: ((b0 + i) // nsb, 0, 0)), full(wxo)],
        out_specs=pl.BlockSpec((bm, d), lambda i: (i, 0)),
        out_shape=jax.ShapeDtypeStruct((rows, d), F32),
        compiler_params=_params("parallel"),
        name="merge_xattn",
    )(x, y_ret, y_mla, y_swa, z, wr, wm, ws, wo, g.reshape(1, d), wq, memkv, wxo)


_PEER_CAND = [(a, b) for a in range(PEER_TOPK) for b in range(PEER_TOPK) if (a + 1) * (b + 1) <= PEER_TOPK]
_PEER_NCAND = len(_PEER_CAND)
_PEER_CAND_ROWS = -(-_PEER_NCAND // 8) * 8


def _top16_rows(sc, vals_ref, idx_ref, lanes, payload=None):
    n = sc.shape[0]
    rows = lax.broadcasted_iota(jnp.int32, sc.shape, 0).astype(F32)
    for r in range(PEER_TOPK):
        m = jnp.max(sc, axis=0, keepdims=True)
        win = jnp.min(jnp.where(sc == m, rows, float(n)), axis=0, keepdims=True)
        hit = rows == win
        vals_ref[pl.ds(r, 1), lanes] = m
        if payload is None:
            idx_ref[pl.ds(r, 1), lanes] = win
        else:
            idx_ref[pl.ds(r, 1), lanes] = jnp.sum(jnp.where(hit, payload, 0.0), axis=0, keepdims=True)
        sc = jnp.where(hit, -jnp.inf, sc)


def _route_scratch(bm):
    tk = pltpu.VMEM((PEER_TOPK, bm), F32)
    cand = pltpu.VMEM((_PEER_CAND_ROWS, bm), F32)
    slot = pltpu.VMEM((PEER_HEADS * PEER_TOPK, bm), F32)
    return [tk, tk, tk, tk, cand, cand, tk, tk, slot, slot, slot]


def _route_scores(h, qh, keys_ref):
    return [lax.dot_general(keys_ref[2 * h + p], qh, (((1,), (1,)), ((), ())),
                            preferred_element_type=F32) for p in range(2)]


def _route_pieces(h, score, n_tokens, v1_ref, n1_ref, v2_ref, n2_ref, cs_ref, ci_ref, bs_ref, bi_ref,
                  gt_ref, i1t_ref, i2t_ref):
    k = PEER_TOPK
    out_rows = pl.ds(h * k if isinstance(h, int) else pl.multiple_of(h * k, k), k)

    def first_stage(p, lanes):
        v_ref, n_ref = ((v1_ref, n1_ref), (v2_ref, n2_ref))[p]
        _top16_rows(score(p, lanes), v_ref, n_ref, lanes)

    def second_stage(lanes):
        for c, (a, b) in enumerate(_PEER_CAND):
            cs_ref[pl.ds(c, 1), lanes] = v1_ref[pl.ds(a, 1), lanes] + v2_ref[pl.ds(b, 1), lanes]
            ci_ref[pl.ds(c, 1), lanes] = n1_ref[pl.ds(a, 1), lanes] * PEER_KEYS + n2_ref[pl.ds(b, 1), lanes]
        _top16_rows(cs_ref[:, lanes], bs_ref, bi_ref, lanes, payload=ci_ref[:, lanes])
        best = bs_ref[:, lanes]
        e = jnp.exp(best - best[0:1, :])
        gt_ref[out_rows, lanes] = e / jnp.sum(e, axis=0, keepdims=True)
        flat = bi_ref[:, lanes]
        first = jnp.floor(flat * (1.0 / PEER_KEYS))
        i1t_ref[out_rows, lanes] = first
        i2t_ref[out_rows, lanes] = flat - first * PEER_KEYS

    pieces = []
    for c in range(n_tokens // LANES):
        lanes = slice(c * LANES, (c + 1) * LANES)
        pieces += [functools.partial(first_stage, 0, lanes), functools.partial(first_stage, 1, lanes),
                   functools.partial(second_stage, lanes)]
    return pieces


def _interleave(a, b):
    i = j = 0
    while i < len(a) or j < len(b):
        if j >= len(b) or (i < len(a) and i * len(b) <= j * len(a)):
            a[i]()
            i += 1
        else:
            b[j]()
            j += 1


def _route_init(cs_ref, ci_ref):
    cs_ref[...] = jnp.full(cs_ref.shape, -jnp.inf, F32)
    ci_ref[...] = jnp.zeros(ci_ref.shape, F32)


def _peer_route_kernel(x_ref, g_ref, wq_ref, keys_ref, hn_ref, gate_ref, i1_ref, i2_ref, *scratch):
    hn = _rms(x_ref[...], g_ref[...]).astype(BF16)
    hn_ref[...] = hn
    q = jnp.dot(hn, wq_ref[...], preferred_element_type=F32).astype(BF16)
    _route_init(scratch[4], scratch[5])
    for h in range(PEER_HEADS):
        scores = _route_scores(h, q[:, h * PEER_DKEY:(h + 1) * PEER_DKEY], keys_ref)
        for piece in _route_pieces(h, lambda p, lanes: scores[p][:, lanes], x_ref.shape[0], *scratch):
            piece()
    gt_ref, i1t_ref, i2t_ref = scratch[-3:]
    gate_ref[...] = gt_ref[...].T
    i1_ref[...] = i1t_ref[...].T
    i2_ref[...] = i2t_ref[...].T


def peer_route(x, g, wq, keys, bm):
    rows, d = x.shape
    slots = PEER_HEADS * PEER_TOPK
    row = lambda width: pl.BlockSpec((bm, width), lambda i: (i, 0))
    return pl.pallas_call(
        _peer_route_kernel,
        grid=(rows // bm,),
        in_specs=[row(d), pl.BlockSpec((1, d), lambda i: (0, 0)),
                  pl.BlockSpec(wq.shape, lambda i: (0, 0)),
                  pl.BlockSpec(keys.shape, lambda i: (0, 0, 0))],
        out_specs=[row(d), row(slots), row(slots), row(slots)],
        out_shape=[jax.ShapeDtypeStruct((rows, d), BF16)] + [jax.ShapeDtypeStruct((rows, slots), F32)] * 3,
        scratch_shapes=_route_scratch(bm),
        compiler_params=_params("parallel"),
        name="peer_route",
    )(x, g.reshape(1, d), wq, keys)


G_PITCH = 132
G_TOKENS_PER_MATMUL = 16
G_COPY_GROUP = 32


def _peer_gates_kernel(gate_ref, i1_ref, i2_ref, o_ref, *s_refs):
    nk = PEER_KEYS
    slots = gate_ref.shape[2]
    sub = G_TOKENS_PER_MATMUL
    grp = G_COPY_GROUP
    key = lax.broadcasted_iota(jnp.int32, (sub, nk, slots), 1).astype(F32)

    def build(gi):
        for c in range(grp // sub):
            tok = pl.ds(gi * grp + c * sub, sub)
            wa = jnp.where(i1_ref[tok] == key, gate_ref[tok], 0.0).astype(BF16)
            wb = jnp.where(i2_ref[tok] == key, 1.0, 0.0).astype(BF16)
            g = lax.dot_general(wa, wb, (((2,), (2,)), ((0,), (0,))),
                                preferred_element_type=F32)
            for u in range(sub):
                s_refs[gi][pl.ds((c * sub + u) * G_PITCH, nk), :] = g[u]

    def gather(gi):
        for i in range(nk):
            o_ref[gi * grp:(gi + 1) * grp, i * nk:(i + 1) * nk] = (
                s_refs[gi][pl.ds(i, grp, stride=G_PITCH), :].astype(o_ref.dtype))

    for gi in range(len(s_refs)):
        build(gi)
        if gi:
            gather(gi - 1)
    gather(len(s_refs) - 1)


def peer_gates(gate, i1, i2, bm):
    t, slots = gate.shape
    n_exp = PEER_KEYS * PEER_KEYS
    spec = pl.BlockSpec((bm, 1, slots), lambda i: (i, 0, 0))
    r3 = lambda a: a.reshape(t, 1, slots)
    return pl.pallas_call(
        _peer_gates_kernel,
        grid=(t // bm,),
        in_specs=[spec, spec, spec],
        out_specs=pl.BlockSpec((bm, n_exp), lambda i: (i, 0)),
        out_shape=jax.ShapeDtypeStruct((t, n_exp), BF16),
        scratch_shapes=[pltpu.VMEM((G_COPY_GROUP * G_PITCH, PEER_KEYS), F32)] * (bm // G_COPY_GROUP),
        compiler_params=_params("parallel"),
        name="peer_gates",
    )(r3(gate), r3(i1), r3(i2))


EXPERT_CHUNKS = 8


def _experts_step(hn_ref, gm_ref, ut_ref, v_ref, o_ref):
    a = jnp.dot(hn_ref[...], ut_ref[...], preferred_element_type=F32)
    act = 0.5 * a * (1.0 + lax.erf(a * (2.0 ** -0.5)))
    p = (act * gm_ref[...].astype(F32)).astype(BF16)
    o_ref[...] += jnp.dot(p, v_ref[...], preferred_element_type=F32)


def _peer_expert_kernel(x_ref, hn_ref, gm_ref, ut_ref, v_ref, o_ref):
    @pl.when(pl.program_id(1) == 0)
    def _():
        o_ref[...] = x_ref[...]

    _experts_step(hn_ref, gm_ref, ut_ref, v_ref, o_ref)


def _peer_fused_kernel(x_ref, hn_ref, gm_ref, ut_ref, v_ref, xr_ref, g_ref, wq_ref, keys_ref,
                       o_ref, hnr_ref, gate_ref, i1_ref, i2_ref, s_ref, *scratch):
    j = pl.program_id(1)

    @pl.when(j == 0)
    def _():
        o_ref[...] = x_ref[...]
        hn = _rms(xr_ref[...], g_ref[...]).astype(BF16)
        hnr_ref[...] = hn
        q = jnp.dot(hn, wq_ref[...], preferred_element_type=F32).astype(BF16)
        for h in range(PEER_HEADS):
            for p, s in enumerate(_route_scores(h, q[:, h * PEER_DKEY:(h + 1) * PEER_DKEY], keys_ref)):
                s_ref[2 * h + p] = s
        _route_init(scratch[4], scratch[5])

    route = _route_pieces(j, lambda p, lanes: s_ref[2 * j + p, :, lanes], xr_ref.shape[0], *scratch)

    def expert_chunk(c):
        width = ut_ref.shape[1] // EXPERT_CHUNKS
        cols = slice(c * width, (c + 1) * width)
        a = jnp.dot(hn_ref[...], ut_ref[:, cols], preferred_element_type=F32)
        act = 0.5 * a * (1.0 + lax.erf(a * (2.0 ** -0.5)))
        p = (act * gm_ref[:, cols].astype(F32)).astype(BF16)
        o_ref[...] += jnp.dot(p, v_ref[cols, :], preferred_element_type=F32)

    _interleave([functools.partial(expert_chunk, c) for c in range(EXPERT_CHUNKS)], route)

    @pl.when(j == PEER_HEADS - 1)
    def _():
        gt_ref, i1t_ref, i2t_ref = scratch[-3:]
        gate_ref[...] = gt_ref[...].T
        i1_ref[...] = i1t_ref[...].T
        i2_ref[...] = i2t_ref[...].T


def peer_experts(x, hn, gm, ut, v, bm, bn, route=None):
    rows, d = x.shape
    n_exp = ut.shape[1]
    blk = lambda width: pl.BlockSpec((bm, width), lambda i, j: (i, 0))
    in_specs = [blk(d), blk(d),
                pl.BlockSpec((bm, bn), lambda i, j: (i, j)),
                pl.BlockSpec((d, bn), lambda i, j: (0, j)),
                pl.BlockSpec((bn, d), lambda i, j: (j, 0))]
    out_specs = [blk(d)]
    out_shape = [jax.ShapeDtypeStruct((rows, d), F32)]
    args = [x, hn, gm, ut, v]
    kern, scratch, name = _peer_expert_kernel, [], "peer_experts"
    if route is not None:
        g, wq, keys, x_next = route
        assert n_exp // bn == PEER_HEADS and x_next.shape == x.shape
        slots = PEER_HEADS * PEER_TOPK
        in_specs += [blk(d), pl.BlockSpec((1, d), lambda i, j: (0, 0)),
                     pl.BlockSpec(wq.shape, lambda i, j: (0, 0)),
                     pl.BlockSpec(keys.shape, lambda i, j: (0, 0, 0))]
        out_specs += [blk(d), blk(slots), blk(slots), blk(slots)]
        out_shape += [jax.ShapeDtypeStruct((rows, d), BF16)] + [jax.ShapeDtypeStruct((rows, slots), F32)] * 3
        args += [x_next, g.reshape(1, d), wq, keys]
        scratch = [pltpu.VMEM((2 * PEER_HEADS, PEER_KEYS, bm), F32)] + _route_scratch(bm)
        kern, name = _peer_fused_kernel, "peer_experts_route"
    out = pl.pallas_call(
        kern,
        grid=(rows // bm, n_exp // bn),
        in_specs=in_specs,
        out_specs=out_specs,
        out_shape=out_shape,
        scratch_shapes=scratch,
        compiler_params=_params("parallel", "arbitrary"),
        name=name,
    )(*args)
    return out[0] if route is None else out


def _final_norm_kernel(x_ref, g_ref, o_ref):
    o_ref[...] = _rms(x_ref[...], g_ref[...])


def final_norm(x, g, bm):
    t, d = x.shape
    return pl.pallas_call(
        _final_norm_kernel,
        grid=(t // bm,),
        in_specs=[pl.BlockSpec((bm, d), lambda i: (i, 0)), pl.BlockSpec((1, d), lambda i: (0, 0))],
        out_specs=pl.BlockSpec((bm, d), lambda i: (i, 0)),
        out_shape=jax.ShapeDtypeStruct((t, d), F32),
        compiler_params=_params("parallel"),
        name="final_norm",
    )(x, g.reshape(1, d))


def _pad_heads(w, heads, dim, axis):
    shape = w.shape[:axis] + (heads, dim) + w.shape[axis + 1:]
    w = w.reshape(shape)
    pad = [(0, 0)] * w.ndim
    pad[axis + 1] = (0, HEAD_PAD - dim)
    w = jnp.pad(w, pad)
    return w.reshape(w.shape[:axis] + (heads * HEAD_PAD,) + w.shape[axis + 2:])


def _rot_half_cols(w):
    half = w.shape[-1] // 2
    return jnp.concatenate([-w[..., half:], w[..., :half]], axis=-1)


def _pack_w_in(w_in):
    rq, rk, rv, rg, qa, kva, kr, sq, sk, sv, gate = jnp.split(
        w_in, [int(p) for p in np.cumsum(
            [512, 512, 512, 512, MLA_Q_RANK, MLA_KV_RANK, MLA_ROPE,
             SWA_HEADS * SWA_DIM, SWA_KV_HEADS * SWA_DIM, SWA_KV_HEADS * SWA_DIM])], axis=-1)
    place = lambda w: jnp.pad(w, ((0, 0), (0, 0), (MLA_NOPE, HEAD_PAD - MLA_NOPE - MLA_ROPE)))
    cols = [rq, rk, rv, rg, _pad_heads(sq, SWA_HEADS, SWA_DIM, 2), gate, qa,
            _pad_heads(sk, SWA_KV_HEADS, SWA_DIM, 2), _pad_heads(sv, SWA_KV_HEADS, SWA_DIM, 2),
            kva, place(kr), place(_rot_half_cols(kr))]
    out = jnp.concatenate(cols, axis=-1).astype(BF16)
    assert out.shape[-1] == Z_COLS
    return out


def _rope_tables(seq, half):
    inv_freq = jnp.power(ROPE_BASE, -jnp.arange(half, dtype=F32) / half)
    ang = jnp.arange(seq, dtype=F32)[:, None] * inv_freq[None, :]
    return jnp.cos(ang), jnp.sin(ang)


def _t5_buckets(rel):
    n = np.maximum(rel, 0)
    max_exact = REL_BUCKETS // 2
    large = max_exact + (np.log(np.maximum(n, 1) / max_exact) / np.log(REL_MAX_DIST / max_exact)
                         * (REL_BUCKETS - max_exact)).astype(np.int32)
    large = np.minimum(large, REL_BUCKETS - 1)
    return np.where(n < max_exact, n, large).astype(np.int32)


def _swa_bias(rel_bias):
    w = SWA_WINDOW
    rel = (np.arange(w)[:, None] + w) - np.arange(2 * w)[None, :]
    bucket = _t5_buckets(rel)
    table = rel_bias.astype(F32).T
    bias = sum(jnp.where(bucket[None] == b, table[:, b][:, None, None], 0.0) for b in range(REL_BUCKETS))
    bias = jnp.where(((rel >= 0) & (rel < w))[None], bias, NEG_INF)
    return bias.reshape(SWA_KV_HEADS, (SWA_HEADS // SWA_KV_HEADS) * w, 2 * w)


def _retention_tables():
    c = RET_CHUNK
    log_gamma = np.log(1.0 - np.exp2(-5.0 - np.arange(RET_HEADS, dtype=np.float64)))
    idx = np.arange(c, dtype=np.float64)
    diff = idx[:, None] - idx[None, :]
    intra = np.where(diff >= 0, np.exp(log_gamma[:, None, None] * np.maximum(diff, 0.0)), 0.0)
    qw = np.exp(log_gamma[:, None] * (idx + 1.0)[None, :])
    kw = np.exp(log_gamma[:, None] * (c - 1.0 - idx)[None, :])
    bc = lambda a: jnp.asarray(np.broadcast_to(a[:, :, None], (RET_HEADS, c, c)), F32)
    chunk_decay = tuple(float(v) for v in np.exp(log_gamma * c))
    return jnp.asarray(intra, F32), bc(qw), bc(kw), chunk_decay


def kernel(x, mem, rel_bias, g_mix, w_in, mla_q_norm, w_mla_qb, mla_kv_norm, w_mla_kvb, swa_sinks,
           w_branch_ret, w_branch_mla, w_branch_swa, w_out, g_xattn, g_mem, w_xq, w_xkv, w_xo,
           g_ffn, w_peer_query, peer_sub_keys, peer_u, peer_v, g_final):
    batch, seq, d = x.shape
    depth = w_in.shape[0]
    t = batch * seq
    mem_len = mem.shape[1]
    bm = min(512, seq)
    big = min(1024, seq)
    n_groups = next(n for n in (4, 2, 1) if batch % n == 0)
    gb = batch // n_groups
    group = gb * seq
    peer_bn = PEER_KEYS * PEER_KEYS // PEER_HEADS

    w_in_p = _pack_w_in(w_in)
    qb = w_mla_qb.reshape(depth, MLA_Q_RANK, MLA_HEADS, MLA_NOPE + MLA_ROPE)
    q_nope, q_rope = qb[..., :MLA_NOPE], qb[..., MLA_NOPE:]
    zpad = jnp.zeros(qb.shape[:3] + (HEAD_PAD - MLA_NOPE - MLA_ROPE,), F32)
    flat = lambda w: w.reshape(w.shape[:2] + (MLA_HEADS * HEAD_PAD,)).astype(BF16)
    w_q1 = flat(jnp.concatenate([q_nope, q_rope, zpad], axis=-1))
    w_q2 = flat(jnp.concatenate([jnp.zeros_like(q_nope), _rot_half_cols(q_rope), zpad], axis=-1))
    kvb = w_mla_kvb.reshape(depth, MLA_KV_RANK, MLA_HEADS, MLA_NOPE + MLA_V)
    w_k = flat(jnp.pad(kvb[..., :MLA_NOPE], ((0, 0),) * 3 + ((0, HEAD_PAD - MLA_NOPE),)))
    w_v = flat(jnp.pad(kvb[..., MLA_NOPE:], ((0, 0),) * 3 + ((0, HEAD_PAD - MLA_V),)))
    w_bret = w_branch_ret.astype(BF16)
    w_bmla = w_branch_mla.astype(BF16)
    w_bswa = _pad_heads(w_branch_swa, SWA_HEADS, SWA_DIM, 1).astype(BF16)
    w_out_b = w_out.astype(BF16)
    w_xq_b, w_xkv_b, w_xo_b = w_xq.astype(BF16), w_xkv.astype(BF16), w_xo.astype(BF16)
    w_pq = w_peer_query.astype(BF16)
    half = PEER_DKEY // 2
    keys = jnp.stack([jnp.pad(peer_sub_keys[:, :, 0], ((0, 0),) * 3 + ((0, half),)),
                      jnp.pad(peer_sub_keys[:, :, 1], ((0, 0),) * 3 + ((half, 0),))], axis=2)
    keys = keys.reshape(depth, 2 * PEER_HEADS, PEER_KEYS, PEER_DKEY).astype(BF16)
    peer_ut = jnp.swapaxes(peer_u, 1, 2).astype(BF16)
    peer_vb = peer_v.astype(BF16)

    cos64, sin64 = _rope_tables(seq, RET_DIM // 2)
    ret_cos = jnp.concatenate([cos64, cos64], axis=1)
    ret_sin = jnp.concatenate([-sin64, sin64], axis=1)
    cos16, sin16 = _rope_tables(seq, MLA_ROPE // 2)
    tail = jnp.zeros((seq, HEAD_PAD - MLA_NOPE - MLA_ROPE), F32)
    mla_cos = jnp.concatenate([jnp.ones((seq, MLA_NOPE), F32), cos16, cos16, tail], axis=1)
    mla_sin = jnp.concatenate([jnp.zeros((seq, MLA_NOPE), F32), sin16, sin16, tail], axis=1)
    intra, ret_qw, ret_kw, chunk_decay = _retention_tables()
    swa_bias = _swa_bias(rel_bias)
    grp = SWA_HEADS // SWA_KV_HEADS

    mem2 = mem.reshape(batch * mem_len, d)
    memkv = [norm_matmul(mem2, g_mem[l], w_xkv_b[l], bm=min(512, batch * mem_len), bn=w_xkv_b.shape[2],
                         out_dtype=BF16).reshape(batch, mem_len, -1) for l in range(depth)]
    sinks = [jnp.broadcast_to(swa_sinks[l].reshape(SWA_KV_HEADS, grp, 1, 1), (SWA_KV_HEADS, grp, SWA_WINDOW, 1)
                              ).reshape(SWA_KV_HEADS, grp * SWA_WINDOW, 1) for l in range(depth)]

    def mixers(xg, l, g):
        z = norm_matmul(xg, g_mix[l], w_in_p[l], bm=big, bn=Z_COLS // 3, out_dtype=BF16)
        y_ret = retention(z, gb, seq, ret_cos, ret_sin, intra, ret_qw, ret_kw, chunk_decay,
                          chunks_per_step=min(4, seq // RET_CHUNK))
        q = mla_q(z, mla_q_norm[l], w_q1[l], w_q2[l], mla_cos, mla_sin, seq, bm)
        k, v = mla_kv(z, mla_kv_norm[l], w_k[l], w_v[l], mla_cos, mla_sin, seq, bm)
        y_mla = mla_attention(q, k, v, gb, seq, bq=big)
        y_swa = swa(z, gb, seq, swa_bias, sinks[l], n_win=min(4, seq // SWA_WINDOW))
        return merge_xattn(xg, y_ret, y_mla, y_swa, z, w_bret[l], w_bmla[l], w_bswa[l], w_out_b[l],
                           g_xattn[l], w_xq_b[l], memkv[l][g * gb:(g + 1) * gb], w_xo_b[l], seq, bm,
                           row0=0, rows=group)

    x = x.reshape(t, d)
    xs = [x[g * group:(g + 1) * group] for g in range(n_groups)]
    stages = [(l, g) for l in range(depth) for g in range(n_groups)]
    route_args = lambda l: (g_ffn[l], w_pq[l], keys[l])
    l0, g0 = stages[0]
    xa = mixers(xs[g0], l0, g0)
    hn, gate, i1, i2 = peer_route(xa, *route_args(l0), bm=min(256, seq))
    for s, (l, g) in enumerate(stages):
        overlap = n_groups > 1 and s + 1 < len(stages)
        if overlap:
            ln, gn = stages[s + 1]
            xa_next = mixers(xs[gn], ln, gn)
        gm = peer_gates(gate, i1, i2, bm=min(256, seq))
        if overlap:
            xs[g], hn, gate, i1, i2 = peer_experts(xa, hn, gm, peer_ut[l], peer_vb[l], bm=bm, bn=peer_bn,
                                                   route=route_args(ln) + (xa_next,))
            xa = xa_next
        else:
            xs[g] = peer_experts(xa, hn, gm, peer_ut[l], peer_vb[l], bm=min(big, group), bn=peer_bn)
            if s + 1 < len(stages):
                ln, gn = stages[s + 1]
                xa = mixers(xs[gn], ln, gn)
                hn, gate, i1, i2 = peer_route(xa, *route_args(ln), bm=min(256, seq))
    out = [final_norm(xg, g_final, bm) for xg in xs]
    return jnp.concatenate(out, axis=0).reshape(batch, seq, d)
```

```python
import functools
import math

import numpy as np
import jax
import jax.numpy as jnp
from jax import lax
from jax.experimental import pallas as pl
from jax.experimental.pallas import tpu as pltpu

F32 = jnp.float32
BF16 = jnp.bfloat16

EPS = 1e-6
NEG_INF = -1e30
ROPE_BASE = 10000.0
N_BRANCH = 3

RET_HEADS = 4
RET_DIM = 128
RET_CHUNK = 128

MLA_HEADS = 8
MLA_NOPE = 64
MLA_ROPE = 32
MLA_V = 64
MLA_Q_RANK = 256
MLA_KV_RANK = 128

SWA_HEADS = 8
SWA_KV_HEADS = 2
SWA_DIM = 64
SWA_WINDOW = 128
REL_BUCKETS = 32
REL_MAX_DIST = 128

XA_HEADS = 4
XA_DIM = 128

PEER_HEADS = 8
PEER_KEYS = 128
PEER_DKEY = 128
PEER_TOPK = 16

LANES = 128
HEAD_PAD = LANES

Z_RQ, Z_RK, Z_RV, Z_RG = 0, 512, 1024, 1536
Z_SQ = 2048
Z_GATE = 3072
Z_QA = 6144
Z_SK = 6400
Z_SV = 6656
Z_KVA = 6912
Z_KRP = 7040
Z_KRR = 7168
Z_COLS = 7296

VMEM_LIMIT = 56 * 1024 * 1024


def _params(*sem):
    return pltpu.CompilerParams(dimension_semantics=sem, vmem_limit_bytes=VMEM_LIMIT)


def _rms(x, g=None):
    y = x * lax.rsqrt(jnp.mean(x * x, axis=-1, keepdims=True) + EPS)
    return y if g is None else y * g


def _norm_matmul_kernel(x_ref, g_ref, w_ref, o_ref, h_ref):
    @pl.when(pl.program_id(1) == 0)
    def _():
        h_ref[...] = _rms(x_ref[...].astype(F32), g_ref[...]).astype(BF16)

    o_ref[...] = jnp.dot(h_ref[...], w_ref[...], preferred_element_type=F32).astype(o_ref.dtype)


def norm_matmul(x, g, w, bm, bn, out_dtype):
    m, k = x.shape
    n = w.shape[1]
    return pl.pallas_call(
        _norm_matmul_kernel,
        grid=(m // bm, n // bn),
        in_specs=[pl.BlockSpec((bm, k), lambda i, j: (i, 0)),
                  pl.BlockSpec((1, k), lambda i, j: (0, 0)),
                  pl.BlockSpec((k, bn), lambda i, j: (0, j))],
        out_specs=pl.BlockSpec((bm, bn), lambda i, j: (i, j)),
        out_shape=jax.ShapeDtypeStruct((m, n), out_dtype),
        scratch_shapes=[pltpu.VMEM((bm, k), BF16)],
        compiler_params=_params("parallel", "arbitrary"),
        name="norm_matmul",
    )(x, g.reshape(1, k), w)


def _retention_kernel(q_ref, k_ref, v_ref, g_ref, cos_ref, sin_ref, intra_ref, qw_ref, kw_ref,
                      o_ref, state_ref, *, chunk_decay):
    @pl.when(pl.program_id(1) == 0)
    def _():
        state_ref[...] = jnp.zeros_like(state_ref)

    half = RET_DIM // 2
    c = RET_CHUNK
    for h in range(RET_HEADS):
        cols = slice(h * RET_DIM, (h + 1) * RET_DIM)
        state = state_ref[h]
        for n in range(q_ref.shape[0] // c):
            rows = slice(n * c, (n + 1) * c)
            cos = cos_ref[rows, :]
            sin = sin_ref[rows, :]
            rope = lambda t: t * cos + pltpu.roll(t, half, 1) * sin
            q = rope(q_ref[rows, cols].astype(F32)).astype(BF16)
            k = rope(k_ref[rows, cols].astype(F32)) * (RET_DIM ** -0.5)
            v = v_ref[rows, cols]
            scores = lax.dot_general(q, k.astype(BF16), (((1,), (1,)), ((), ())),
                                     preferred_element_type=F32) * intra_ref[h]
            inner = jnp.dot(scores.astype(BF16), v, preferred_element_type=F32)
            cross = jnp.dot(q, state.astype(BF16), preferred_element_type=F32) * qw_ref[h]
            kv = lax.dot_general((k * kw_ref[h]).astype(BF16), v, (((0,), (0,)), ((), ())),
                                 preferred_element_type=F32)
            state = chunk_decay[h] * state + kv
            y = _rms(inner + cross)
            gate = g_ref[rows, cols].astype(F32)
            o_ref[rows, cols] = (y * (gate * jax.nn.sigmoid(gate))).astype(o_ref.dtype)
        state_ref[h] = state


def retention(z, batch, seq, cos, sin, intra, qw, kw, chunk_decay, chunks_per_step):
    c = RET_CHUNK
    rows = c * chunks_per_step
    nc = seq // rows
    w = RET_HEADS * RET_DIM
    zspec = lambda col: pl.BlockSpec((rows, w), lambda b, n, col=col: (b * nc + n, col // w))
    tab = pl.BlockSpec((rows, RET_DIM), lambda b, n: (n, 0))
    const = pl.BlockSpec((RET_HEADS, c, c), lambda b, n: (0, 0, 0))
    return pl.pallas_call(
        functools.partial(_retention_kernel, chunk_decay=chunk_decay),
        grid=(batch, nc),
        in_specs=[zspec(Z_RQ), zspec(Z_RK), zspec(Z_RV), zspec(Z_RG), tab, tab, const, const, const],
        out_specs=pl.BlockSpec((rows, w), lambda b, n: (b * nc + n, 0)),
        out_shape=jax.ShapeDtypeStruct((batch * seq, w), BF16),
        scratch_shapes=[pltpu.VMEM((RET_HEADS, RET_DIM, RET_DIM), F32)],
        compiler_params=_params("parallel", "arbitrary"),
        name="retention",
    )(z, z, z, z, cos, sin, intra, qw, kw)


def _mla_q_kernel(qa_ref, g_ref, w1_ref, w2_ref, cos_ref, sin_ref, o_ref):
    qn = _rms(qa_ref[...].astype(F32), g_ref[...]).astype(BF16)
    a = jnp.dot(qn, w1_ref[...], preferred_element_type=F32)
    b = jnp.dot(qn, w2_ref[...], preferred_element_type=F32)
    cos = cos_ref[...]
    sin = sin_ref[...]
    scale = (MLA_NOPE + MLA_ROPE) ** -0.5 * math.log2(math.e)
    for h in range(MLA_HEADS):
        cols = slice(h * HEAD_PAD, (h + 1) * HEAD_PAD)
        o_ref[:, cols] = ((a[:, cols] * cos + b[:, cols] * sin) * scale).astype(o_ref.dtype)


def mla_q(z, g, w1, w2, cos, sin, seq, bm):
    t = z.shape[0]
    wd = MLA_HEADS * HEAD_PAD
    nsb = seq // bm
    return pl.pallas_call(
        _mla_q_kernel,
        grid=(t // bm,),
        in_specs=[pl.BlockSpec((bm, MLA_Q_RANK), lambda i: (i, Z_QA // MLA_Q_RANK)),
                  pl.BlockSpec((1, MLA_Q_RANK), lambda i: (0, 0)),
                  pl.BlockSpec((MLA_Q_RANK, wd), lambda i: (0, 0)),
                  pl.BlockSpec((MLA_Q_RANK, wd), lambda i: (0, 0)),
                  pl.BlockSpec((bm, HEAD_PAD), lambda i: (i % nsb, 0)),
                  pl.BlockSpec((bm, HEAD_PAD), lambda i: (i % nsb, 0))],
        out_specs=pl.BlockSpec((bm, wd), lambda i: (i, 0)),
        out_shape=jax.ShapeDtypeStruct((t, wd), BF16),
        compiler_params=_params("parallel"),
        name="mla_q",
    )(z, g.reshape(1, -1), w1, w2, cos, sin)


def _mla_kv_kernel(kva_ref, krp_ref, krr_ref, g_ref, wk_ref, wv_ref, cos_ref, sin_ref, k_ref, v_ref):
    kvn = _rms(kva_ref[...].astype(F32), g_ref[...]).astype(BF16)
    k = jnp.dot(kvn, wk_ref[...], preferred_element_type=F32)
    krope = krp_ref[...].astype(F32) * cos_ref[...] + krr_ref[...].astype(F32) * sin_ref[...]
    for h in range(MLA_HEADS):
        cols = slice(h * HEAD_PAD, (h + 1) * HEAD_PAD)
        k_ref[:, cols] = (k[:, cols] + krope).astype(k_ref.dtype)
    v_ref[...] = jnp.dot(kvn, wv_ref[...], preferred_element_type=F32).astype(v_ref.dtype)


def mla_kv(z, g, wk, wv, cos, sin, seq, bm):
    t = z.shape[0]
    wd = MLA_HEADS * HEAD_PAD
    nsb = seq // bm
    zs = lambda col: pl.BlockSpec((bm, LANES), lambda i, col=col: (i, col // LANES))
    tab = pl.BlockSpec((bm, HEAD_PAD), lambda i: (i % nsb, 0))
    wspec = pl.BlockSpec((MLA_KV_RANK, wd), lambda i: (0, 0))
    out = pl.BlockSpec((bm, wd), lambda i: (i, 0))
    return pl.pallas_call(
        _mla_kv_kernel,
        grid=(t // bm,),
        in_specs=[zs(Z_KVA), zs(Z_KRP), zs(Z_KRR), pl.BlockSpec((1, MLA_KV_RANK), lambda i: (0, 0)),
                  wspec, wspec, tab, tab],
        out_specs=[out, out],
        out_shape=[jax.ShapeDtypeStruct((t, wd), BF16)] * 2,
        compiler_params=_params("parallel"),
        name="mla_kv",
    )(z, z, z, g.reshape(1, -1), wk, wv, cos, sin)


def _mla_attn_kernel(q_ref, k_ref, v_ref, o_ref, *, bq, n_split, heads):
    qi = pl.program_id(2)
    rows = bq // n_split
    chains = [(h, r) for h in range(heads) for r in range(n_split)]
    cols = lambda h: slice(h * HEAD_PAD, (h + 1) * HEAD_PAD)
    qs = [q_ref[pl.ds(r * rows, rows), cols(h)] for h, r in chains]

    def step(j, carry, diagonal):
        start = pl.multiple_of(j * bq, bq)
        out = []
        for c, (h, r) in enumerate(chains):
            m, l, acc = carry[c]
            n_keys = (r + 1) * rows if diagonal else bq
            kb = k_ref[pl.ds(start, n_keys), cols(h)]
            vb = v_ref[pl.ds(start, n_keys), cols(h)]
            s = lax.dot_general(qs[c], kb, (((1,), (1,)), ((), ())), preferred_element_type=F32)
            if diagonal:
                row = r * rows + lax.broadcasted_iota(jnp.int32, (rows, n_keys), 0)
                col = lax.broadcasted_iota(jnp.int32, (rows, n_keys), 1)
                s = jnp.where(col <= row, s, NEG_INF)
            m_new = jnp.maximum(m, jnp.max(s, axis=-1, keepdims=True))
            alpha = jnp.exp2(m - m_new)
            p = jnp.exp2(s - m_new)
            l = alpha * l + jnp.sum(p, axis=-1, keepdims=True)
            acc = alpha * acc + jnp.dot(p.astype(BF16), vb, preferred_element_type=F32)
            out.append((m_new, l, acc))
        return tuple(out)

    init = tuple((jnp.full((rows, 1), NEG_INF, F32), jnp.zeros((rows, 1), F32),
                  jnp.zeros((rows, HEAD_PAD), F32)) for _ in chains)
    carry = lax.fori_loop(0, qi, functools.partial(step, diagonal=False), init)
    carry = step(qi, carry, diagonal=True)
    per_block = HEAD_PAD // MLA_V
    for r in range(n_split):
        outs = []
        for h in range(heads):
            _, l, acc = carry[chains.index((h, r))]
            outs.append((acc / l)[:, :MLA_V])
        for b in range(heads // per_block):
            o_ref[pl.ds(r * rows, rows), cols(b)] = jnp.concatenate(
                outs[b * per_block:(b + 1) * per_block], axis=1).astype(o_ref.dtype)


def mla_attention(q, k, v, batch, seq, bq, n_split=2, heads=2):
    nq = seq // bq
    width = heads * HEAD_PAD
    qspec = pl.BlockSpec((bq, width), lambda b, h, i: (b * nq + i, h))
    kvspec = pl.BlockSpec((seq, width), lambda b, h, i: (b, h))
    return pl.pallas_call(
        functools.partial(_mla_attn_kernel, bq=bq, n_split=n_split, heads=heads),
        grid=(batch, MLA_HEADS // heads, nq),
        in_specs=[qspec, kvspec, kvspec],
        out_specs=pl.BlockSpec((bq, heads * MLA_V), lambda b, h, i: (b * nq + i, h)),
        out_shape=jax.ShapeDtypeStruct((q.shape[0], MLA_HEADS * MLA_V), BF16),
        compiler_params=_params("parallel", "parallel", "arbitrary"),
        name="mla_attention",
    )(q, k, v)


def _swa_kernel(q_ref, kc_ref, kp_ref, vc_ref, vp_ref, bias_ref, sink_ref, o_ref, *, n_win):
    w = SWA_WINDOW
    grp = SWA_HEADS // SWA_KV_HEADS
    first = pl.program_id(1) == 0
    prev_half = lax.broadcasted_iota(jnp.int32, (grp * w, 2 * w), 1) < w
    for r in range(n_win):
        rows = slice(r * w, (r + 1) * w)
        for g in range(SWA_KV_HEADS):
            kcols = slice(g * HEAD_PAD, (g + 1) * HEAD_PAD)
            if r == 0:
                kb = jnp.concatenate([kp_ref[:, kcols], kc_ref[0:w, kcols]], axis=0)
                vb = jnp.concatenate([vp_ref[:, kcols], vc_ref[0:w, kcols]], axis=0)
            else:
                kb = kc_ref[(r - 1) * w:(r + 1) * w, kcols]
                vb = vc_ref[(r - 1) * w:(r + 1) * w, kcols]
            q = jnp.concatenate(
                [q_ref[rows, (g * grp + j) * HEAD_PAD:(g * grp + j + 1) * HEAD_PAD] for j in range(grp)], axis=0)
            s = lax.dot_general(q, kb, (((1,), (1,)), ((), ())), preferred_element_type=F32)
            s = s * (SWA_DIM ** -0.5) + bias_ref[g]
            if r == 0:
                s = jnp.where(jnp.logical_and(first, prev_half), NEG_INF, s)
            sink = sink_ref[g]
            m = jnp.maximum(jnp.max(s, axis=-1, keepdims=True), sink)
            e = jnp.exp(s - m)
            p = e / (jnp.sum(e, axis=-1, keepdims=True) + jnp.exp(sink - m))
            out = jnp.dot(p.astype(BF16), vb, preferred_element_type=F32)
            for j in range(grp):
                o_ref[rows, (g * grp + j) * HEAD_PAD:(g * grp + j + 1) * HEAD_PAD] = (
                    out[j * w:(j + 1) * w].astype(o_ref.dtype))


def swa(z, batch, seq, bias, sinks, n_win):
    w = SWA_WINDOW
    rows = n_win * w
    ns = seq // rows
    qw = SWA_HEADS * HEAD_PAD
    kw = SWA_KV_HEADS * HEAD_PAD
    grp = SWA_HEADS // SWA_KV_HEADS
    cur = lambda col: pl.BlockSpec((rows, kw), lambda b, n, col=col: (b * ns + n, col // kw))
    prev = lambda col: pl.BlockSpec(
        (w, kw), lambda b, n, col=col: ((b * ns + n) * n_win - jnp.minimum(n, 1), col // kw))
    return pl.pallas_call(
        functools.partial(_swa_kernel, n_win=n_win),
        grid=(batch, ns),
        in_specs=[pl.BlockSpec((rows, qw), lambda b, n: (b * ns + n, Z_SQ // qw)),
                  cur(Z_SK), prev(Z_SK), cur(Z_SV), prev(Z_SV),
                  pl.BlockSpec((SWA_KV_HEADS, grp * w, 2 * w), lambda b, n: (0, 0, 0)),
                  pl.BlockSpec((SWA_KV_HEADS, grp * w, 1), lambda b, n: (0, 0, 0))],
        out_specs=pl.BlockSpec((rows, qw), lambda b, n: (b * ns + n, 0)),
        out_shape=jax.ShapeDtypeStruct((batch * seq, qw), BF16),
        compiler_params=_params("parallel", "arbitrary"),
        name="swa",
    )(z, z, z, z, z, bias, sinks)


def _merge_xattn_kernel(x_ref, yr_ref, ym_ref, ys_ref, gl_ref, wr_ref, wm_ref, ws_ref, wo_ref,
                        g_ref, wq_ref, kv_ref, wxo_ref, o_ref):
    d = x_ref.shape[1]
    merged = None
    for i, (y_ref, w_ref) in enumerate(((yr_ref, wr_ref), (ym_ref, wm_ref), (ys_ref, ws_ref))):
        gate = jax.nn.sigmoid(gl_ref[:, i * d:(i + 1) * d].astype(F32))
        term = gate * jnp.dot(y_ref[...], w_ref[...], preferred_element_type=F32)
        merged = term if merged is None else merged + term
    x = x_ref[...] + jnp.dot(merged.astype(BF16), wo_ref[...], preferred_element_type=F32)

    hn = _rms(x, g_ref[...]).astype(BF16)
    q = (jnp.dot(hn, wq_ref[...], preferred_element_type=F32) * (XA_DIM ** -0.5)).astype(BF16)
    width = XA_HEADS * XA_DIM
    outs = []
    for h in range(XA_HEADS):
        cols = slice(h * XA_DIM, (h + 1) * XA_DIM)
        k = kv_ref[0, :, cols]
        v = kv_ref[0, :, width + h * XA_DIM:width + (h + 1) * XA_DIM]
        s = lax.dot_general(q[:, cols], k, (((1,), (1,)), ((), ())), preferred_element_type=F32)
        e = jnp.exp(s - jnp.max(s, axis=-1, keepdims=True))
        p = e / jnp.sum(e, axis=-1, keepdims=True)
        outs.append(jnp.dot(p.astype(BF16), v, preferred_element_type=F32).astype(BF16))
    o = jnp.concatenate(outs, axis=1)
    o_ref[...] = x + jnp.dot(o, wxo_ref[...], preferred_element_type=F32)


def merge_xattn(x, y_ret, y_mla, y_swa, z, wr, wm, ws, wo, g, wq, memkv, wxo, seq, bm, row0, rows):
    d = x.shape[1]
    nsb = seq // bm
    b0 = row0 // bm
    row = lambda width: pl.BlockSpec((bm, width), lambda i: (b0 + i, 0))
    full = lambda a: pl.BlockSpec(a.shape, lambda i: (0, 0))
    return pl.pallas_call(
        _merge_xattn_kernel,
        grid=(rows // bm,),
        in_specs=[row(d), row(y_ret.shape[1]), row(y_mla.shape[1]), row(y_swa.shape[1]),
                  pl.BlockSpec((bm, N_BRANCH * d), lambda i: (b0 + i, Z_GATE // (N_BRANCH * d))),
                  full(wr), full(wm), full(ws), full(wo),
                  pl.BlockSpec((1, d), lambda i: (0, 0)), full(wq),
                  pl.BlockSpec((1,) + memkv.shape[1:], lambda i: ((b0 + i) // nsb, 0, 0)), full(wxo)],
        out_specs=pl.BlockSpec((bm, d), lambda i: (i, 0)),
        out_shape=jax.ShapeDtypeStruct((rows, d), F32),
        compiler_params=_params("parallel"),
        name="merge_xattn",
    )(x, y_ret, y_mla, y_swa, z, wr, wm, ws, wo, g.reshape(1, d), wq, memkv, wxo)


_PEER_CAND = [(a, b) for a in range(PEER_TOPK) for b in range(PEER_TOPK) if (a + 1) * (b + 1) <= PEER_TOPK]
_PEER_NCAND = len(_PEER_CAND)
_PEER_CAND_ROWS = -(-_PEER_NCAND // 8) * 8


def _top16_rows(sc, vals_ref, idx_ref, lanes, payload=None):
    n = sc.shape[0]
    rows = lax.broadcasted_iota(jnp.int32, sc.shape, 0).astype(F32)
    for r in range(PEER_TOPK):
        m = jnp.max(sc, axis=0, keepdims=True)
        win = jnp.min(jnp.where(sc == m, rows, float(n)), axis=0, keepdims=True)
        hit = rows == win
        vals_ref[pl.ds(r, 1), lanes] = m
        if payload is None:
            idx_ref[pl.ds(r, 1), lanes] = win
        else:
            idx_ref[pl.ds(r, 1), lanes] = jnp.sum(jnp.where(hit, payload, 0.0), axis=0, keepdims=True)
        sc = jnp.where(hit, -jnp.inf, sc)


def _route_scratch(bm):
    tk = pltpu.VMEM((PEER_TOPK, bm), F32)
    cand = pltpu.VMEM((_PEER_CAND_ROWS, bm), F32)
    slot = pltpu.VMEM((PEER_HEADS * PEER_TOPK, bm), F32)
    return [tk, tk, tk, tk, cand, cand, tk, tk, slot, slot, slot]


def _route_scores(h, qh, keys_ref):
    return [lax.dot_general(keys_ref[2 * h + p], qh, (((1,), (1,)), ((), ())),
                            preferred_element_type=F32) for p in range(2)]


def _route_pieces(h, score, n_tokens, v1_ref, n1_ref, v2_ref, n2_ref, cs_ref, ci_ref, bs_ref, bi_ref,
                  gt_ref, i1t_ref, i2t_ref):
    k = PEER_TOPK
    out_rows = pl.ds(h * k if isinstance(h, int) else pl.multiple_of(h * k, k), k)

    def first_stage(p, lanes):
        v_ref, n_ref = ((v1_ref, n1_ref), (v2_ref, n2_ref))[p]
        _top16_rows(score(p, lanes), v_ref, n_ref, lanes)

    def second_stage(lanes):
        for c, (a, b) in enumerate(_PEER_CAND):
            cs_ref[pl.ds(c, 1), lanes] = v1_ref[pl.ds(a, 1), lanes] + v2_ref[pl.ds(b, 1), lanes]
            ci_ref[pl.ds(c, 1), lanes] = n1_ref[pl.ds(a, 1), lanes] * PEER_KEYS + n2_ref[pl.ds(b, 1), lanes]
        _top16_rows(cs_ref[:, lanes], bs_ref, bi_ref, lanes, payload=ci_ref[:, lanes])
        best = bs_ref[:, lanes]
        e = jnp.exp(best - best[0:1, :])
        gt_ref[out_rows, lanes] = e / jnp.sum(e, axis=0, keepdims=True)
        flat = bi_ref[:, lanes]
        first = jnp.floor(flat * (1.0 / PEER_KEYS))
        i1t_ref[out_rows, lanes] = first
        i2t_ref[out_rows, lanes] = flat - first * PEER_KEYS

    pieces = []
    for c in range(n_tokens // LANES):
        lanes = slice(c * LANES, (c + 1) * LANES)
        pieces += [functools.partial(first_stage, 0, lanes), functools.partial(first_stage, 1, lanes),
                   functools.partial(second_stage, lanes)]
    return pieces


def _interleave(a, b):
    i = j = 0
    while i < len(a) or j < len(b):
        if j >= len(b) or (i < len(a) and i * len(b) <= j * len(a)):
            a[i]()
            i += 1
        else:
            b[j]()
            j += 1


def _route_init(cs_ref, ci_ref):
    cs_ref[...] = jnp.full(cs_ref.shape, -jnp.inf, F32)
    ci_ref[...] = jnp.zeros(ci_ref.shape, F32)


def _peer_route_kernel(x_ref, g_ref, wq_ref, keys_ref, hn_ref, gate_ref, i1_ref, i2_ref, *scratch):
    hn = _rms(x_ref[...], g_ref[...]).astype(BF16)
    hn_ref[...] = hn
    q = jnp.dot(hn, wq_ref[...], preferred_element_type=F32).astype(BF16)
    _route_init(scratch[4], scratch[5])
    for h in range(PEER_HEADS):
        scores = _route_scores(h, q[:, h * PEER_DKEY:(h + 1) * PEER_DKEY], keys_ref)
        for piece in _route_pieces(h, lambda p, lanes: scores[p][:, lanes], x_ref.shape[0], *scratch):
            piece()
    gt_ref, i1t_ref, i2t_ref = scratch[-3:]
    gate_ref[...] = gt_ref[...].T
    i1_ref[...] = i1t_ref[...].T
    i2_ref[...] = i2t_ref[...].T


def peer_route(x, g, wq, keys, bm):
    rows, d = x.shape
    slots = PEER_HEADS * PEER_TOPK
    row = lambda width: pl.BlockSpec((bm, width), lambda i: (i, 0))
    return pl.pallas_call(
        _peer_route_kernel,
        grid=(rows // bm,),
        in_specs=[row(d), pl.BlockSpec((1, d), lambda i: (0, 0)),
                  pl.BlockSpec(wq.shape, lambda i: (0, 0)),
                  pl.BlockSpec(keys.shape, lambda i: (0, 0, 0))],
        out_specs=[row(d), row(slots), row(slots), row(slots)],
        out_shape=[jax.ShapeDtypeStruct((rows, d), BF16)] + [jax.ShapeDtypeStruct((rows, slots), F32)] * 3,
        scratch_shapes=_route_scratch(bm),
        compiler_params=_params("parallel"),
        name="peer_route",
    )(x, g.reshape(1, d), wq, keys)


G_PITCH = 132
G_TOKENS_PER_MATMUL = 16
G_COPY_GROUP = 32


def _peer_gates_kernel(gate_ref, i1_ref, i2_ref, o_ref, *s_refs):
    nk = PEER_KEYS
    slots = gate_ref.shape[2]
    sub = G_TOKENS_PER_MATMUL
    grp = G_COPY_GROUP
    key = lax.broadcasted_iota(jnp.int32, (sub, nk, slots), 1).astype(F32)

    def build(gi):
        for c in range(grp // sub):
            tok = pl.ds(gi * grp + c * sub, sub)
            wa = jnp.where(i1_ref[tok] == key, 0.5 * gate_ref[tok], 0.0).astype(BF16)
            wb = jnp.where(i2_ref[tok] == key, 1.0, 0.0).astype(BF16)
            g = lax.dot_general(wa, wb, (((2,), (2,)), ((0,), (0,))),
                                preferred_element_type=F32)
            for u in range(sub):
                s_refs[gi][pl.ds((c * sub + u) * G_PITCH, nk), :] = g[u]

    def gather(gi):
        for i in range(nk):
            o_ref[gi * grp:(gi + 1) * grp, i * nk:(i + 1) * nk] = (
                s_refs[gi][pl.ds(i, grp, stride=G_PITCH), :].astype(o_ref.dtype))

    for gi in range(len(s_refs)):
        build(gi)
        if gi:
            gather(gi - 1)
    gather(len(s_refs) - 1)


def peer_gates(gate, i1, i2, bm):
    t, slots = gate.shape
    n_exp = PEER_KEYS * PEER_KEYS
    spec = pl.BlockSpec((bm, 1, slots), lambda i: (i, 0, 0))
    r3 = lambda a: a.reshape(t, 1, slots)
    return pl.pallas_call(
        _peer_gates_kernel,
        grid=(t // bm,),
        in_specs=[spec, spec, spec],
        out_specs=pl.BlockSpec((bm, n_exp), lambda i: (i, 0)),
        out_shape=jax.ShapeDtypeStruct((t, n_exp), BF16),
        scratch_shapes=[pltpu.VMEM((G_COPY_GROUP * G_PITCH, PEER_KEYS), F32)] * (bm // G_COPY_GROUP),
        compiler_params=_params("parallel"),
        name="peer_gates",
    )(r3(gate), r3(i1), r3(i2))


EXPERT_CHUNKS = 8
OUT_CHUNKS = 4


def _gated_gelu(a, half_gates):
    return (a * (1.0 + lax.erf(a * (2.0 ** -0.5)))).astype(BF16) * half_gates


def _experts_step(hn_ref, gm_ref, ut_ref, v_ref, o_ref):
    a = jnp.dot(hn_ref[...], ut_ref[...], preferred_element_type=F32)
    p = _gated_gelu(a, gm_ref[...])
    o_ref[...] += jnp.dot(p, v_ref[...], preferred_element_type=F32)


def _peer_expert_kernel(x_ref, hn_ref, gm_ref, ut_ref, v_ref, o_ref):
    @pl.when(pl.program_id(1) == 0)
    def _():
        o_ref[...] = x_ref[...]

    _experts_step(hn_ref, gm_ref, ut_ref, v_ref, o_ref)


def _peer_fused_kernel(x_ref, hn_ref, gm_ref, ut_ref, v_ref, xr_ref, g_ref, wq_ref, keys_ref,
                       o_ref, hnr_ref, gate_ref, i1_ref, i2_ref, p_ref, s_ref, *scratch):
    j = pl.program_id(1)

    @pl.when(j == 0)
    def _():
        o_ref[...] = x_ref[...]
        hn = _rms(xr_ref[...], g_ref[...]).astype(BF16)
        hnr_ref[...] = hn
        q = jnp.dot(hn, wq_ref[...], preferred_element_type=F32).astype(BF16)
        for h in range(PEER_HEADS):
            for p, s in enumerate(_route_scores(h, q[:, h * PEER_DKEY:(h + 1) * PEER_DKEY], keys_ref)):
                s_ref[2 * h + p] = s
        _route_init(scratch[4], scratch[5])

    route = _route_pieces(j, lambda p, lanes: s_ref[2 * j + p, :, lanes], xr_ref.shape[0], *scratch)

    def activation_chunk(c):
        width = ut_ref.shape[1] // EXPERT_CHUNKS
        cols = slice(c * width, (c + 1) * width)
        a = jnp.dot(hn_ref[...], ut_ref[:, cols], preferred_element_type=F32)
        p_ref[:, cols] = _gated_gelu(a, gm_ref[:, cols])

    def output_chunk(c):
        width = o_ref.shape[1] // OUT_CHUNKS
        cols = slice(c * width, (c + 1) * width)
        o_ref[:, cols] += jnp.dot(p_ref[...], v_ref[:, cols], preferred_element_type=F32)

    experts = ([functools.partial(activation_chunk, c) for c in range(EXPERT_CHUNKS)]
               + [functools.partial(output_chunk, c) for c in range(OUT_CHUNKS)])
    _interleave(experts, route)

    @pl.when(j == PEER_HEADS - 1)
    def _():
        gt_ref, i1t_ref, i2t_ref = scratch[-3:]
        gate_ref[...] = gt_ref[...].T
        i1_ref[...] = i1t_ref[...].T
        i2_ref[...] = i2t_ref[...].T


def peer_experts(x, hn, gm, ut, v, bm, bn, route=None):
    rows, d = x.shape
    n_exp = ut.shape[1]
    blk = lambda width: pl.BlockSpec((bm, width), lambda i, j: (i, 0))
    in_specs = [blk(d), blk(d),
                pl.BlockSpec((bm, bn), lambda i, j: (i, j)),
                pl.BlockSpec((d, bn), lambda i, j: (0, j)),
                pl.BlockSpec((bn, d), lambda i, j: (j, 0))]
    out_specs = [blk(d)]
    out_shape = [jax.ShapeDtypeStruct((rows, d), F32)]
    args = [x, hn, gm, ut, v]
    kern, scratch, name = _peer_expert_kernel, [], "peer_experts"
    if route is not None:
        g, wq, keys, x_next = route
        assert n_exp // bn == PEER_HEADS and x_next.shape == x.shape
        slots = PEER_HEADS * PEER_TOPK
        in_specs += [blk(d), pl.BlockSpec((1, d), lambda i, j: (0, 0)),
                     pl.BlockSpec(wq.shape, lambda i, j: (0, 0)),
                     pl.BlockSpec(keys.shape, lambda i, j: (0, 0, 0))]
        out_specs += [blk(d), blk(slots), blk(slots), blk(slots)]
        out_shape += [jax.ShapeDtypeStruct((rows, d), BF16)] + [jax.ShapeDtypeStruct((rows, slots), F32)] * 3
        args += [x_next, g.reshape(1, d), wq, keys]
        scratch = [pltpu.VMEM((bm, bn), BF16),
                   pltpu.VMEM((2 * PEER_HEADS, PEER_KEYS, bm), F32)] + _route_scratch(bm)
        kern, name = _peer_fused_kernel, "peer_experts_route"
    out = pl.pallas_call(
        kern,
        grid=(rows // bm, n_exp // bn),
        in_specs=in_specs,
        out_specs=out_specs,
        out_shape=out_shape,
        scratch_shapes=scratch,
        compiler_params=_params("parallel", "arbitrary"),
        name=name,
    )(*args)
    return out[0] if route is None else out


def _final_norm_kernel(x_ref, g_ref, o_ref):
    o_ref[...] = _rms(x_ref[...], g_ref[...])


def final_norm(x, g, bm):
    t, d = x.shape
    return pl.pallas_call(
        _final_norm_kernel,
        grid=(t // bm,),
        in_specs=[pl.BlockSpec((bm, d), lambda i: (i, 0)), pl.BlockSpec((1, d), lambda i: (0, 0))],
        out_specs=pl.BlockSpec((bm, d), lambda i: (i, 0)),
        out_shape=jax.ShapeDtypeStruct((t, d), F32),
        compiler_params=_params("parallel"),
        name="final_norm",
    )(x, g.reshape(1, d))


def _pad_heads(w, heads, dim, axis):
    shape = w.shape[:axis] + (heads, dim) + w.shape[axis + 1:]
    w = w.reshape(shape)
    pad = [(0, 0)] * w.ndim
    pad[axis + 1] = (0, HEAD_PAD - dim)
    w = jnp.pad(w, pad)
    return w.reshape(w.shape[:axis] + (heads * HEAD_PAD,) + w.shape[axis + 2:])


def _rot_half_cols(w):
    half = w.shape[-1] // 2
    return jnp.concatenate([-w[..., half:], w[..., :half]], axis=-1)


def _pack_w_in(w_in):
    rq, rk, rv, rg, qa, kva, kr, sq, sk, sv, gate = jnp.split(
        w_in, [int(p) for p in np.cumsum(
            [512, 512, 512, 512, MLA_Q_RANK, MLA_KV_RANK, MLA_ROPE,
             SWA_HEADS * SWA_DIM, SWA_KV_HEADS * SWA_DIM, SWA_KV_HEADS * SWA_DIM])], axis=-1)
    place = lambda w: jnp.pad(w, ((0, 0), (0, 0), (MLA_NOPE, HEAD_PAD - MLA_NOPE - MLA_ROPE)))
    cols = [rq, rk, rv, rg, _pad_heads(sq, SWA_HEADS, SWA_DIM, 2), gate, qa,
            _pad_heads(sk, SWA_KV_HEADS, SWA_DIM, 2), _pad_heads(sv, SWA_KV_HEADS, SWA_DIM, 2),
            kva, place(kr), place(_rot_half_cols(kr))]
    out = jnp.concatenate(cols, axis=-1).astype(BF16)
    assert out.shape[-1] == Z_COLS
    return out


def _rope_tables(seq, half):
    inv_freq = jnp.power(ROPE_BASE, -jnp.arange(half, dtype=F32) / half)
    ang = jnp.arange(seq, dtype=F32)[:, None] * inv_freq[None, :]
    return jnp.cos(ang), jnp.sin(ang)


def _t5_buckets(rel):
    n = np.maximum(rel, 0)
    max_exact = REL_BUCKETS // 2
    large = max_exact + (np.log(np.maximum(n, 1) / max_exact) / np.log(REL_MAX_DIST / max_exact)
                         * (REL_BUCKETS - max_exact)).astype(np.int32)
    large = np.minimum(large, REL_BUCKETS - 1)
    return np.where(n < max_exact, n, large).astype(np.int32)


def _swa_bias(rel_bias):
    w = SWA_WINDOW
    rel = (np.arange(w)[:, None] + w) - np.arange(2 * w)[None, :]
    bucket = _t5_buckets(rel)
    table = rel_bias.astype(F32).T
    bias = sum(jnp.where(bucket[None] == b, table[:, b][:, None, None], 0.0) for b in range(REL_BUCKETS))
    bias = jnp.where(((rel >= 0) & (rel < w))[None], bias, NEG_INF)
    return bias.reshape(SWA_KV_HEADS, (SWA_HEADS // SWA_KV_HEADS) * w, 2 * w)


def _retention_tables():
    c = RET_CHUNK
    log_gamma = np.log(1.0 - np.exp2(-5.0 - np.arange(RET_HEADS, dtype=np.float64)))
    idx = np.arange(c, dtype=np.float64)
    diff = idx[:, None] - idx[None, :]
    intra = np.where(diff >= 0, np.exp(log_gamma[:, None, None] * np.maximum(diff, 0.0)), 0.0)
    qw = np.exp(log_gamma[:, None] * (idx + 1.0)[None, :])
    kw = np.exp(log_gamma[:, None] * (c - 1.0 - idx)[None, :])
    bc = lambda a: jnp.asarray(np.broadcast_to(a[:, :, None], (RET_HEADS, c, c)), F32)
    chunk_decay = tuple(float(v) for v in np.exp(log_gamma * c))
    return jnp.asarray(intra, F32), bc(qw), bc(kw), chunk_decay


def kernel(x, mem, rel_bias, g_mix, w_in, mla_q_norm, w_mla_qb, mla_kv_norm, w_mla_kvb, swa_sinks,
           w_branch_ret, w_branch_mla, w_branch_swa, w_out, g_xattn, g_mem, w_xq, w_xkv, w_xo,
           g_ffn, w_peer_query, peer_sub_keys, peer_u, peer_v, g_final):
    batch, seq, d = x.shape
    depth = w_in.shape[0]
    t = batch * seq
    mem_len = mem.shape[1]
    bm = min(512, seq)
    big = min(1024, seq)
    n_groups = next(n for n in (4, 2, 1) if batch % n == 0)
    gb = batch // n_groups
    group = gb * seq
    peer_bn = PEER_KEYS * PEER_KEYS // PEER_HEADS

    w_in_p = _pack_w_in(w_in)
    qb = w_mla_qb.reshape(depth, MLA_Q_RANK, MLA_HEADS, MLA_NOPE + MLA_ROPE)
    q_nope, q_rope = qb[..., :MLA_NOPE], qb[..., MLA_NOPE:]
    zpad = jnp.zeros(qb.shape[:3] + (HEAD_PAD - MLA_NOPE - MLA_ROPE,), F32)
    flat = lambda w: w.reshape(w.shape[:2] + (MLA_HEADS * HEAD_PAD,)).astype(BF16)
    w_q1 = flat(jnp.concatenate([q_nope, q_rope, zpad], axis=-1))
    w_q2 = flat(jnp.concatenate([jnp.zeros_like(q_nope), _rot_half_cols(q_rope), zpad], axis=-1))
    kvb = w_mla_kvb.reshape(depth, MLA_KV_RANK, MLA_HEADS, MLA_NOPE + MLA_V)
    w_k = flat(jnp.pad(kvb[..., :MLA_NOPE], ((0, 0),) * 3 + ((0, HEAD_PAD - MLA_NOPE),)))
    w_v = flat(jnp.pad(kvb[..., MLA_NOPE:], ((0, 0),) * 3 + ((0, HEAD_PAD - MLA_V),)))
    w_bret = w_branch_ret.astype(BF16)
    w_bmla = w_branch_mla.astype(BF16)
    w_bswa = _pad_heads(w_branch_swa, SWA_HEADS, SWA_DIM, 1).astype(BF16)
    w_out_b = w_out.astype(BF16)
    w_xq_b, w_xkv_b, w_xo_b = w_xq.astype(BF16), w_xkv.astype(BF16), w_xo.astype(BF16)
    w_pq = w_peer_query.astype(BF16)
    half = PEER_DKEY // 2
    keys = jnp.stack([jnp.pad(peer_sub_keys[:, :, 0], ((0, 0),) * 3 + ((0, half),)),
                      jnp.pad(peer_sub_keys[:, :, 1], ((0, 0),) * 3 + ((half, 0),))], axis=2)
    keys = keys.reshape(depth, 2 * PEER_HEADS, PEER_KEYS, PEER_DKEY).astype(BF16)
    peer_ut = jnp.swapaxes(peer_u, 1, 2).astype(BF16)
    peer_vb = peer_v.astype(BF16)

    cos64, sin64 = _rope_tables(seq, RET_DIM // 2)
    ret_cos = jnp.concatenate([cos64, cos64], axis=1)
    ret_sin = jnp.concatenate([-sin64, sin64], axis=1)
    cos16, sin16 = _rope_tables(seq, MLA_ROPE // 2)
    tail = jnp.zeros((seq, HEAD_PAD - MLA_NOPE - MLA_ROPE), F32)
    mla_cos = jnp.concatenate([jnp.ones((seq, MLA_NOPE), F32), cos16, cos16, tail], axis=1)
    mla_sin = jnp.concatenate([jnp.zeros((seq, MLA_NOPE), F32), sin16, sin16, tail], axis=1)
    intra, ret_qw, ret_kw, chunk_decay = _retention_tables()
    swa_bias = _swa_bias(rel_bias)
    grp = SWA_HEADS // SWA_KV_HEADS

    mem2 = mem.reshape(batch * mem_len, d)
    memkv = [norm_matmul(mem2, g_mem[l], w_xkv_b[l], bm=min(512, batch * mem_len), bn=w_xkv_b.shape[2],
                         out_dtype=BF16).reshape(batch, mem_len, -1) for l in range(depth)]
    sinks = [jnp.broadcast_to(swa_sinks[l].reshape(SWA_KV_HEADS, grp, 1, 1), (SWA_KV_HEADS, grp, SWA_WINDOW, 1)
                              ).reshape(SWA_KV_HEADS, grp * SWA_WINDOW, 1) for l in range(depth)]

    def mixers(xg, l, g):
        z = norm_matmul(xg, g_mix[l], w_in_p[l], bm=big, bn=Z_COLS // 3, out_dtype=BF16)
        y_ret = retention(z, gb, seq, ret_cos, ret_sin, intra, ret_qw, ret_kw, chunk_decay,
                          chunks_per_step=min(4, seq // RET_CHUNK))
        q = mla_q(z, mla_q_norm[l], w_q1[l], w_q2[l], mla_cos, mla_sin, seq, bm)
        k, v = mla_kv(z, mla_kv_norm[l], w_k[l], w_v[l], mla_cos, mla_sin, seq, bm)
        y_mla = mla_attention(q, k, v, gb, seq, bq=big)
        y_swa = swa(z, gb, seq, swa_bias, sinks[l], n_win=min(4, seq // SWA_WINDOW))
        return merge_xattn(xg, y_ret, y_mla, y_swa, z, w_bret[l], w_bmla[l], w_bswa[l], w_out_b[l],
                           g_xattn[l], w_xq_b[l], memkv[l][g * gb:(g + 1) * gb], w_xo_b[l], seq, bm,
                           row0=0, rows=group)

    x = x.reshape(t, d)
    xs = [x[g * group:(g + 1) * group] for g in range(n_groups)]
    stages = [(l, g) for l in range(depth) for g in range(n_groups)]
    route_args = lambda l: (g_ffn[l], w_pq[l], keys[l])
    l0, g0 = stages[0]
    xa = mixers(xs[g0], l0, g0)
    hn, gate, i1, i2 = peer_route(xa, *route_args(l0), bm=min(256, seq))
    for s, (l, g) in enumerate(stages):
        overlap = n_groups > 1 and s + 1 < len(stages)
        if overlap:
            ln, gn = stages[s + 1]
            xa_next = mixers(xs[gn], ln, gn)
        gm = peer_gates(gate, i1, i2, bm=min(256, seq))
        if overlap:
            xs[g], hn, gate, i1, i2 = peer_experts(xa, hn, gm, peer_ut[l], peer_vb[l], bm=bm, bn=peer_bn,
                                                   route=route_args(ln) + (xa_next,))
            xa = xa_next
        else:
            xs[g] = peer_experts(xa, hn, gm, peer_ut[l], peer_vb[l], bm=min(big, group), bn=peer_bn)
            if s + 1 < len(stages):
                ln, gn = stages[s + 1]
                xa = mixers(xs[gn], ln, gn)
                hn, gate, i1, i2 = peer_route(xa, *route_args(ln), bm=min(256, seq))
    out = [final_norm(xg, g_final, bm) for xg in xs]
    return jnp.concatenate(out, axis=0).reshape(batch, seq, d)
```

```python
import functools
import math

import numpy as np
import jax
import jax.numpy as jnp
from jax import lax
from jax.experimental import pallas as pl
from jax.experimental.pallas import tpu as pltpu

F32 = jnp.float32
BF16 = jnp.bfloat16

EPS = 1e-6
NEG_INF = -1e30
ROPE_BASE = 10000.0
N_BRANCH = 3

RET_HEADS = 4
RET_DIM = 128
RET_CHUNK = 128

MLA_HEADS = 8
MLA_NOPE = 64
MLA_ROPE = 32
MLA_V = 64
MLA_Q_RANK = 256
MLA_KV_RANK = 128

SWA_HEADS = 8
SWA_KV_HEADS = 2
SWA_DIM = 64
SWA_WINDOW = 128
REL_BUCKETS = 32
REL_MAX_DIST = 128

XA_HEADS = 4
XA_DIM = 128

PEER_HEADS = 8
PEER_KEYS = 128
PEER_DKEY = 128
PEER_TOPK = 16

LANES = 128
HEAD_PAD = LANES

Z_RQ, Z_RK, Z_RV, Z_RG = 0, 512, 1024, 1536
Z_SQ = 2048
Z_GATE = 3072
Z_QA = 6144
Z_SK = 6400
Z_SV = 6656
Z_KVA = 6912
Z_KRP = 7040
Z_KRR = 7168
Z_COLS = 7296

VMEM_LIMIT = 56 * 1024 * 1024


def _params(*sem):
    return pltpu.CompilerParams(dimension_semantics=sem, vmem_limit_bytes=VMEM_LIMIT)


def _rms(x, g=None):
    y = x * lax.rsqrt(jnp.mean(x * x, axis=-1, keepdims=True) + EPS)
    return y if g is None else y * g


def _norm_matmul_kernel(x_ref, g_ref, w_ref, o_ref, h_ref):
    @pl.when(pl.program_id(1) == 0)
    def _():
        h_ref[...] = _rms(x_ref[...].astype(F32), g_ref[...]).astype(BF16)

    o_ref[...] = jnp.dot(h_ref[...], w_ref[...], preferred_element_type=F32).astype(o_ref.dtype)


def norm_matmul(x, g, w, bm, bn, out_dtype):
    m, k = x.shape
    n = w.shape[1]
    return pl.pallas_call(
        _norm_matmul_kernel,
        grid=(m // bm, n // bn),
        in_specs=[pl.BlockSpec((bm, k), lambda i, j: (i, 0)),
                  pl.BlockSpec((1, k), lambda i, j: (0, 0)),
                  pl.BlockSpec((k, bn), lambda i, j: (0, j))],
        out_specs=pl.BlockSpec((bm, bn), lambda i, j: (i, j)),
        out_shape=jax.ShapeDtypeStruct((m, n), out_dtype),
        scratch_shapes=[pltpu.VMEM((bm, k), BF16)],
        compiler_params=_params("parallel", "arbitrary"),
        name="norm_matmul",
    )(x, g.reshape(1, k), w)


def _retention_kernel(q_ref, k_ref, v_ref, g_ref, cos_ref, sin_ref, intra_ref, qw_ref, kw_ref,
                      o_ref, state_ref, *, chunk_decay):
    @pl.when(pl.program_id(1) == 0)
    def _():
        state_ref[...] = jnp.zeros_like(state_ref)

    half = RET_DIM // 2
    c = RET_CHUNK
    for h in range(RET_HEADS):
        cols = slice(h * RET_DIM, (h + 1) * RET_DIM)
        state = state_ref[h]
        for n in range(q_ref.shape[0] // c):
            rows = slice(n * c, (n + 1) * c)
            cos = cos_ref[rows, :]
            sin = sin_ref[rows, :]
            rope = lambda t: t * cos + pltpu.roll(t, half, 1) * sin
            q = rope(q_ref[rows, cols].astype(F32)).astype(BF16)
            k = rope(k_ref[rows, cols].astype(F32)) * (RET_DIM ** -0.5)
            v = v_ref[rows, cols]
            scores = lax.dot_general(q, k.astype(BF16), (((1,), (1,)), ((), ())),
                                     preferred_element_type=F32) * intra_ref[h]
            inner = jnp.dot(scores.astype(BF16), v, preferred_element_type=F32)
            cross = jnp.dot(q, state.astype(BF16), preferred_element_type=F32) * qw_ref[h]
            kv = lax.dot_general((k * kw_ref[h]).astype(BF16), v, (((0,), (0,)), ((), ())),
                                 preferred_element_type=F32)
            state = chunk_decay[h] * state + kv
            y = _rms(inner + cross)
            gate = g_ref[rows, cols].astype(F32)
            o_ref[rows, cols] = (y * (gate * jax.nn.sigmoid(gate))).astype(o_ref.dtype)
        state_ref[h] = state


def retention(z, batch, seq, cos, sin, intra, qw, kw, chunk_decay, chunks_per_step):
    c = RET_CHUNK
    rows = c * chunks_per_step
    nc = seq // rows
    w = RET_HEADS * RET_DIM
    zspec = lambda col: pl.BlockSpec((rows, w), lambda b, n, col=col: (b * nc + n, col // w))
    tab = pl.BlockSpec((rows, RET_DIM), lambda b, n: (n, 0))
    const = pl.BlockSpec((RET_HEADS, c, c), lambda b, n: (0, 0, 0))
    return pl.pallas_call(
        functools.partial(_retention_kernel, chunk_decay=chunk_decay),
        grid=(batch, nc),
        in_specs=[zspec(Z_RQ), zspec(Z_RK), zspec(Z_RV), zspec(Z_RG), tab, tab, const, const, const],
        out_specs=pl.BlockSpec((rows, w), lambda b, n: (b * nc + n, 0)),
        out_shape=jax.ShapeDtypeStruct((batch * seq, w), BF16),
        scratch_shapes=[pltpu.VMEM((RET_HEADS, RET_DIM, RET_DIM), F32)],
        compiler_params=_params("parallel", "arbitrary"),
        name="retention",
    )(z, z, z, z, cos, sin, intra, qw, kw)


def _mla_q_kernel(qa_ref, g_ref, w1_ref, w2_ref, cos_ref, sin_ref, o_ref):
    qn = _rms(qa_ref[...].astype(F32), g_ref[...]).astype(BF16)
    a = jnp.dot(qn, w1_ref[...], preferred_element_type=F32)
    b = jnp.dot(qn, w2_ref[...], preferred_element_type=F32)
    cos = cos_ref[...]
    sin = sin_ref[...]
    scale = (MLA_NOPE + MLA_ROPE) ** -0.5 * math.log2(math.e)
    for h in range(MLA_HEADS):
        cols = slice(h * HEAD_PAD, (h + 1) * HEAD_PAD)
        o_ref[:, cols] = ((a[:, cols] * cos + b[:, cols] * sin) * scale).astype(o_ref.dtype)


def mla_q(z, g, w1, w2, cos, sin, seq, bm):
    t = z.shape[0]
    wd = MLA_HEADS * HEAD_PAD
    nsb = seq // bm
    return pl.pallas_call(
        _mla_q_kernel,
        grid=(t // bm,),
        in_specs=[pl.BlockSpec((bm, MLA_Q_RANK), lambda i: (i, Z_QA // MLA_Q_RANK)),
                  pl.BlockSpec((1, MLA_Q_RANK), lambda i: (0, 0)),
                  pl.BlockSpec((MLA_Q_RANK, wd), lambda i: (0, 0)),
                  pl.BlockSpec((MLA_Q_RANK, wd), lambda i: (0, 0)),
                  pl.BlockSpec((bm, HEAD_PAD), lambda i: (i % nsb, 0)),
                  pl.BlockSpec((bm, HEAD_PAD), lambda i: (i % nsb, 0))],
        out_specs=pl.BlockSpec((bm, wd), lambda i: (i, 0)),
        out_shape=jax.ShapeDtypeStruct((t, wd), BF16),
        compiler_params=_params("parallel"),
        name="mla_q",
    )(z, g.reshape(1, -1), w1, w2, cos, sin)


def _mla_kv_kernel(kva_ref, krp_ref, krr_ref, g_ref, wk_ref, wv_ref, cos_ref, sin_ref, k_ref, v_ref):
    kvn = _rms(kva_ref[...].astype(F32), g_ref[...]).astype(BF16)
    k = jnp.dot(kvn, wk_ref[...], preferred_element_type=F32)
    krope = krp_ref[...].astype(F32) * cos_ref[...] + krr_ref[...].astype(F32) * sin_ref[...]
    for h in range(MLA_HEADS):
        cols = slice(h * HEAD_PAD, (h + 1) * HEAD_PAD)
        k_ref[:, cols] = (k[:, cols] + krope).astype(k_ref.dtype)
    v_ref[...] = jnp.dot(kvn, wv_ref[...], preferred_element_type=F32).astype(v_ref.dtype)


def mla_kv(z, g, wk, wv, cos, sin, seq, bm):
    t = z.shape[0]
    wd = MLA_HEADS * HEAD_PAD
    nsb = seq // bm
    zs = lambda col: pl.BlockSpec((bm, LANES), lambda i, col=col: (i, col // LANES))
    tab = pl.BlockSpec((bm, HEAD_PAD), lambda i: (i % nsb, 0))
    wspec = pl.BlockSpec((MLA_KV_RANK, wd), lambda i: (0, 0))
    out = pl.BlockSpec((bm, wd), lambda i: (i, 0))
    return pl.pallas_call(
        _mla_kv_kernel,
        grid=(t // bm,),
        in_specs=[zs(Z_KVA), zs(Z_KRP), zs(Z_KRR), pl.BlockSpec((1, MLA_KV_RANK), lambda i: (0, 0)),
                  wspec, wspec, tab, tab],
        out_specs=[out, out],
        out_shape=[jax.ShapeDtypeStruct((t, wd), BF16)] * 2,
        compiler_params=_params("parallel"),
        name="mla_kv",
    )(z, z, z, g.reshape(1, -1), wk, wv, cos, sin)


def _mla_proj_kernel(qa_ref, kva_ref, krp_ref, krr_ref, gq_ref, w1_ref, w2_ref, gkv_ref, wk_ref, wv_ref,
                     cos_ref, sin_ref, q_ref, k_ref, v_ref):
    _mla_q_kernel(qa_ref, gq_ref, w1_ref, w2_ref, cos_ref, sin_ref, q_ref)
    _mla_kv_kernel(kva_ref, krp_ref, krr_ref, gkv_ref, wk_ref, wv_ref, cos_ref, sin_ref, k_ref, v_ref)


def mla_proj(z, gq, w1, w2, gkv, wk, wv, cos, sin, seq, bm):
    t = z.shape[0]
    wd = MLA_HEADS * HEAD_PAD
    nsb = seq // bm
    zs = lambda col: pl.BlockSpec((bm, LANES), lambda i, col=col: (i, col // LANES))
    tab = pl.BlockSpec((bm, HEAD_PAD), lambda i: (i % nsb, 0))
    full = lambda a: pl.BlockSpec(a.shape, lambda i: (0, 0))
    out = pl.BlockSpec((bm, wd), lambda i: (i, 0))
    gq, gkv = gq.reshape(1, -1), gkv.reshape(1, -1)
    return pl.pallas_call(
        _mla_proj_kernel,
        grid=(t // bm,),
        in_specs=[pl.BlockSpec((bm, MLA_Q_RANK), lambda i: (i, Z_QA // MLA_Q_RANK)),
                  zs(Z_KVA), zs(Z_KRP), zs(Z_KRR), full(gq), full(w1), full(w2), full(gkv), full(wk), full(wv),
                  tab, tab],
        out_specs=[out, out, out],
        out_shape=[jax.ShapeDtypeStruct((t, wd), BF16)] * 3,
        compiler_params=_params("parallel"),
        name="mla_proj",
    )(z, z, z, z, gq, w1, w2, gkv, wk, wv, cos, sin)


def _mla_attn_kernel(q_ref, k_ref, v_ref, o_ref, *, bq, n_split, heads):
    qi = pl.program_id(2)
    rows = bq // n_split
    chains = [(h, r) for h in range(heads) for r in range(n_split)]
    cols = lambda h: slice(h * HEAD_PAD, (h + 1) * HEAD_PAD)
    qs = [q_ref[pl.ds(r * rows, rows), cols(h)] for h, r in chains]

    def step(j, carry, diagonal):
        start = pl.multiple_of(j * bq, bq)
        out = []
        for c, (h, r) in enumerate(chains):
            m, l, acc = carry[c]
            n_keys = (r + 1) * rows if diagonal else bq
            kb = k_ref[pl.ds(start, n_keys), cols(h)]
            vb = v_ref[pl.ds(start, n_keys), cols(h)]
            s = lax.dot_general(qs[c], kb, (((1,), (1,)), ((), ())), preferred_element_type=F32)
            if diagonal:
                row = r * rows + lax.broadcasted_iota(jnp.int32, (rows, n_keys), 0)
                col = lax.broadcasted_iota(jnp.int32, (rows, n_keys), 1)
                s = jnp.where(col <= row, s, NEG_INF)
            m_new = jnp.maximum(m, jnp.max(s, axis=-1, keepdims=True))
            alpha = jnp.exp2(m - m_new)
            p = jnp.exp2(s - m_new)
            l = alpha * l + jnp.sum(p, axis=-1, keepdims=True)
            acc = alpha * acc + jnp.dot(p.astype(BF16), vb, preferred_element_type=F32)
            out.append((m_new, l, acc))
        return tuple(out)

    init = tuple((jnp.full((rows, 1), NEG_INF, F32), jnp.zeros((rows, 1), F32),
                  jnp.zeros((rows, HEAD_PAD), F32)) for _ in chains)
    carry = lax.fori_loop(0, qi, functools.partial(step, diagonal=False), init)
    carry = step(qi, carry, diagonal=True)
    per_block = HEAD_PAD // MLA_V
    for r in range(n_split):
        outs = []
        for h in range(heads):
            _, l, acc = carry[chains.index((h, r))]
            outs.append((acc / l)[:, :MLA_V])
        for b in range(heads // per_block):
            o_ref[pl.ds(r * rows, rows), cols(b)] = jnp.concatenate(
                outs[b * per_block:(b + 1) * per_block], axis=1).astype(o_ref.dtype)


def mla_attention(q, k, v, batch, seq, bq, n_split=2, heads=2):
    nq = seq // bq
    width = heads * HEAD_PAD
    qspec = pl.BlockSpec((bq, width), lambda b, h, i: (b * nq + i, h))
    kvspec = pl.BlockSpec((seq, width), lambda b, h, i: (b, h))
    return pl.pallas_call(
        functools.partial(_mla_attn_kernel, bq=bq, n_split=n_split, heads=heads),
        grid=(batch, MLA_HEADS // heads, nq),
        in_specs=[qspec, kvspec, kvspec],
        out_specs=pl.BlockSpec((bq, heads * MLA_V), lambda b, h, i: (b * nq + i, h)),
        out_shape=jax.ShapeDtypeStruct((q.shape[0], MLA_HEADS * MLA_V), BF16),
        compiler_params=_params("parallel", "parallel", "arbitrary"),
        name="mla_attention",
    )(q, k, v)


def _swa_kernel(q_ref, kc_ref, kp_ref, vc_ref, vp_ref, bias_ref, sink_ref, o_ref, *, n_win):
    w = SWA_WINDOW
    grp = SWA_HEADS // SWA_KV_HEADS
    first = pl.program_id(1) == 0
    prev_half = lax.broadcasted_iota(jnp.int32, (grp * w, 2 * w), 1) < w
    for r in range(n_win):
        rows = slice(r * w, (r + 1) * w)
        for g in range(SWA_KV_HEADS):
            kcols = slice(g * HEAD_PAD, (g + 1) * HEAD_PAD)
            if r == 0:
                kb = jnp.concatenate([kp_ref[:, kcols], kc_ref[0:w, kcols]], axis=0)
                vb = jnp.concatenate([vp_ref[:, kcols], vc_ref[0:w, kcols]], axis=0)
            else:
                kb = kc_ref[(r - 1) * w:(r + 1) * w, kcols]
                vb = vc_ref[(r - 1) * w:(r + 1) * w, kcols]
            q = jnp.concatenate(
                [q_ref[rows, (g * grp + j) * HEAD_PAD:(g * grp + j + 1) * HEAD_PAD] for j in range(grp)], axis=0)
            s = lax.dot_general(q, kb, (((1,), (1,)), ((), ())), preferred_element_type=F32)
            s = s * (SWA_DIM ** -0.5) + bias_ref[g]
            if r == 0:
                s = jnp.where(jnp.logical_and(first, prev_half), NEG_INF, s)
            sink = sink_ref[g]
            m = jnp.maximum(jnp.max(s, axis=-1, keepdims=True), sink)
            e = jnp.exp(s - m)
            p = e / (jnp.sum(e, axis=-1, keepdims=True) + jnp.exp(sink - m))
            out = jnp.dot(p.astype(BF16), vb, preferred_element_type=F32)
            for j in range(grp):
                o_ref[rows, (g * grp + j) * HEAD_PAD:(g * grp + j + 1) * HEAD_PAD] = (
                    out[j * w:(j + 1) * w].astype(o_ref.dtype))


def swa(z, batch, seq, bias, sinks, n_win):
    w = SWA_WINDOW
    rows = n_win * w
    ns = seq // rows
    qw = SWA_HEADS * HEAD_PAD
    kw = SWA_KV_HEADS * HEAD_PAD
    grp = SWA_HEADS // SWA_KV_HEADS
    cur = lambda col: pl.BlockSpec((rows, kw), lambda b, n, col=col: (b * ns + n, col // kw))
    prev = lambda col: pl.BlockSpec(
        (w, kw), lambda b, n, col=col: ((b * ns + n) * n_win - jnp.minimum(n, 1), col // kw))
    return pl.pallas_call(
        functools.partial(_swa_kernel, n_win=n_win),
        grid=(batch, ns),
        in_specs=[pl.BlockSpec((rows, qw), lambda b, n: (b * ns + n, Z_SQ // qw)),
                  cur(Z_SK), prev(Z_SK), cur(Z_SV), prev(Z_SV),
                  pl.BlockSpec((SWA_KV_HEADS, grp * w, 2 * w), lambda b, n: (0, 0, 0)),
                  pl.BlockSpec((SWA_KV_HEADS, grp * w, 1), lambda b, n: (0, 0, 0))],
        out_specs=pl.BlockSpec((rows, qw), lambda b, n: (b * ns + n, 0)),
        out_shape=jax.ShapeDtypeStruct((batch * seq, qw), BF16),
        compiler_params=_params("parallel", "arbitrary"),
        name="swa",
    )(z, z, z, z, z, bias, sinks)


def _merge_xattn_kernel(x_ref, yr_ref, ym_ref, ys_ref, gl_ref, wr_ref, wm_ref, ws_ref, wo_ref,
                        g_ref, wq_ref, kv_ref, wxo_ref, o_ref):
    d = x_ref.shape[1]
    merged = None
    for i, (y_ref, w_ref) in enumerate(((yr_ref, wr_ref), (ym_ref, wm_ref), (ys_ref, ws_ref))):
        gate = jax.nn.sigmoid(gl_ref[:, i * d:(i + 1) * d].astype(F32))
        term = gate * jnp.dot(y_ref[...], w_ref[...], preferred_element_type=F32)
        merged = term if merged is None else merged + term
    x = x_ref[...] + jnp.dot(merged.astype(BF16), wo_ref[...], preferred_element_type=F32)

    hn = _rms(x, g_ref[...]).astype(BF16)
    q = (jnp.dot(hn, wq_ref[...], preferred_element_type=F32) * (XA_DIM ** -0.5)).astype(BF16)
    width = XA_HEADS * XA_DIM
    outs = []
    for h in range(XA_HEADS):
        cols = slice(h * XA_DIM, (h + 1) * XA_DIM)
        k = kv_ref[0, :, cols]
        v = kv_ref[0, :, width + h * XA_DIM:width + (h + 1) * XA_DIM]
        s = lax.dot_general(q[:, cols], k, (((1,), (1,)), ((), ())), preferred_element_type=F32)
        e = jnp.exp(s - jnp.max(s, axis=-1, keepdims=True))
        p = e / jnp.sum(e, axis=-1, keepdims=True)
        outs.append(jnp.dot(p.astype(BF16), v, preferred_element_type=F32).astype(BF16))
    o = jnp.concatenate(outs, axis=1)
    o_ref[...] = x + jnp.dot(o, wxo_ref[...], preferred_element_type=F32)


def merge_xattn(x, y_ret, y_mla, y_swa, z, wr, wm, ws, wo, g, wq, memkv, wxo, seq, bm, row0, rows):
    d = x.shape[1]
    nsb = seq // bm
    b0 = row0 // bm
    row = lambda width: pl.BlockSpec((bm, width), lambda i: (b0 + i, 0))
    full = lambda a: pl.BlockSpec(a.shape, lambda i: (0, 0))
    return pl.pallas_call(
        _merge_xattn_kernel,
        grid=(rows // bm,),
        in_specs=[row(d), row(y_ret.shape[1]), row(y_mla.shape[1]), row(y_swa.shape[1]),
                  pl.BlockSpec((bm, N_BRANCH * d), lambda i: (b0 + i, Z_GATE // (N_BRANCH * d))),
                  full(wr), full(wm), full(ws), full(wo),
                  pl.BlockSpec((1, d), lambda i: (0, 0)), full(wq),
                  pl.BlockSpec((1,) + memkv.shape[1:], lambda i: ((b0 + i) // nsb, 0, 0)), full(wxo)],
        out_specs=pl.BlockSpec((bm, d), lambda i: (i, 0)),
        out_shape=jax.ShapeDtypeStruct((rows, d), F32),
        compiler_params=_params("parallel"),
        name="merge_xattn",
    )(x, y_ret, y_mla, y_swa, z, wr, wm, ws, wo, g.reshape(1, d), wq, memkv, wxo)


_PEER_CAND = [(a, b) for a in range(PEER_TOPK) for b in range(PEER_TOPK) if (a + 1) * (b + 1) <= PEER_TOPK]
_PEER_NCAND = len(_PEER_CAND)
_PEER_CAND_ROWS = -(-_PEER_NCAND // 8) * 8


def _top16_rows(sc, vals_ref, idx_ref, lanes, payload=None):
    n = sc.shape[0]
    rows = lax.broadcasted_iota(jnp.int32, sc.shape, 0).astype(F32)
    for r in range(PEER_TOPK):
        m = jnp.max(sc, axis=0, keepdims=True)
        win = jnp.min(jnp.where(sc == m, rows, float(n)), axis=0, keepdims=True)
        hit = rows == win
        vals_ref[pl.ds(r, 1), lanes] = m
        if payload is None:
            idx_ref[pl.ds(r, 1), lanes] = win
        else:
            idx_ref[pl.ds(r, 1), lanes] = jnp.sum(jnp.where(hit, payload, 0.0), axis=0, keepdims=True)
        sc = jnp.where(hit, -jnp.inf, sc)


def _route_scratch(bm):
    tk = pltpu.VMEM((PEER_TOPK, bm), F32)
    cand = pltpu.VMEM((_PEER_CAND_ROWS, bm), F32)
    slot = pltpu.VMEM((PEER_HEADS * PEER_TOPK, bm), F32)
    return [tk, tk, tk, tk, cand, cand, tk, tk, slot, slot, slot]


def _route_scores(h, qh, keys_ref):
    return [lax.dot_general(keys_ref[2 * h + p], qh, (((1,), (1,)), ((), ())),
                            preferred_element_type=F32) for p in range(2)]


def _route_pieces(h, score, n_tokens, v1_ref, n1_ref, v2_ref, n2_ref, cs_ref, ci_ref, bs_ref, bi_ref,
                  gt_ref, i1t_ref, i2t_ref):
    k = PEER_TOPK
    out_rows = pl.ds(h * k if isinstance(h, int) else pl.multiple_of(h * k, k), k)

    def first_stage(p, lanes):
        v_ref, n_ref = ((v1_ref, n1_ref), (v2_ref, n2_ref))[p]
        _top16_rows(score(p, lanes), v_ref, n_ref, lanes)

    def second_stage(lanes):
        for c, (a, b) in enumerate(_PEER_CAND):
            cs_ref[pl.ds(c, 1), lanes] = v1_ref[pl.ds(a, 1), lanes] + v2_ref[pl.ds(b, 1), lanes]
            ci_ref[pl.ds(c, 1), lanes] = n1_ref[pl.ds(a, 1), lanes] * PEER_KEYS + n2_ref[pl.ds(b, 1), lanes]
        _top16_rows(cs_ref[:, lanes], bs_ref, bi_ref, lanes, payload=ci_ref[:, lanes])
        best = bs_ref[:, lanes]
        e = jnp.exp(best - best[0:1, :])
        gt_ref[out_rows, lanes] = e / jnp.sum(e, axis=0, keepdims=True)
        flat = bi_ref[:, lanes]
        first = jnp.floor(flat * (1.0 / PEER_KEYS))
        i1t_ref[out_rows, lanes] = first
        i2t_ref[out_rows, lanes] = flat - first * PEER_KEYS

    pieces = []
    for c in range(n_tokens // LANES):
        lanes = slice(c * LANES, (c + 1) * LANES)
        pieces += [functools.partial(first_stage, 0, lanes), functools.partial(first_stage, 1, lanes),
                   functools.partial(second_stage, lanes)]
    return pieces


def _interleave(a, b):
    i = j = 0
    while i < len(a) or j < len(b):
        if j >= len(b) or (i < len(a) and i * len(b) <= j * len(a)):
            a[i]()
            i += 1
        else:
            b[j]()
            j += 1


def _route_init(cs_ref, ci_ref):
    cs_ref[...] = jnp.full(cs_ref.shape, -jnp.inf, F32)
    ci_ref[...] = jnp.zeros(ci_ref.shape, F32)


def _peer_route_kernel(x_ref, g_ref, wq_ref, keys_ref, hn_ref, gate_ref, i1_ref, i2_ref, *scratch):
    hn = _rms(x_ref[...], g_ref[...]).astype(BF16)
    hn_ref[...] = hn
    q = jnp.dot(hn, wq_ref[...], preferred_element_type=F32).astype(BF16)
    _route_init(scratch[4], scratch[5])
    for h in range(PEER_HEADS):
        scores = _route_scores(h, q[:, h * PEER_DKEY:(h + 1) * PEER_DKEY], keys_ref)
        for piece in _route_pieces(h, lambda p, lanes: scores[p][:, lanes], x_ref.shape[0], *scratch):
            piece()
    gt_ref, i1t_ref, i2t_ref = scratch[-3:]
    gate_ref[...] = gt_ref[...].T
    i1_ref[...] = i1t_ref[...].T
    i2_ref[...] = i2t_ref[...].T


def peer_route(x, g, wq, keys, bm):
    rows, d = x.shape
    slots = PEER_HEADS * PEER_TOPK
    row = lambda width: pl.BlockSpec((bm, width), lambda i: (i, 0))
    return pl.pallas_call(
        _peer_route_kernel,
        grid=(rows // bm,),
        in_specs=[row(d), pl.BlockSpec((1, d), lambda i: (0, 0)),
                  pl.BlockSpec(wq.shape, lambda i: (0, 0)),
                  pl.BlockSpec(keys.shape, lambda i: (0, 0, 0))],
        out_specs=[row(d), row(slots), row(slots), row(slots)],
        out_shape=[jax.ShapeDtypeStruct((rows, d), BF16)] + [jax.ShapeDtypeStruct((rows, slots), F32)] * 3,
        scratch_shapes=_route_scratch(bm),
        compiler_params=_params("parallel"),
        name="peer_route",
    )(x, g.reshape(1, d), wq, keys)


G_PITCH = 132
G_TOKENS_PER_MATMUL = 16
G_COPY_GROUP = 32


def _peer_gates_kernel(gate_ref, i1_ref, i2_ref, o_ref, *s_refs):
    nk = PEER_KEYS
    slots = gate_ref.shape[2]
    sub = G_TOKENS_PER_MATMUL
    grp = G_COPY_GROUP
    key = lax.broadcasted_iota(jnp.int32, (sub, nk, slots), 1).astype(F32)

    def build(gi):
        for c in range(grp // sub):
            tok = pl.ds(gi * grp + c * sub, sub)
            wa = jnp.where(i1_ref[tok] == key, 0.5 * gate_ref[tok], 0.0).astype(BF16)
            wb = jnp.where(i2_ref[tok] == key, 1.0, 0.0).astype(BF16)
            g = lax.dot_general(wa, wb, (((2,), (2,)), ((0,), (0,))),
                                preferred_element_type=F32)
            for u in range(sub):
                s_refs[gi][pl.ds((c * sub + u) * G_PITCH, nk), :] = g[u]

    def gather(gi):
        for i in range(nk):
            o_ref[gi * grp:(gi + 1) * grp, i * nk:(i + 1) * nk] = (
                s_refs[gi][pl.ds(i, grp, stride=G_PITCH), :].astype(o_ref.dtype))

    for gi in range(len(s_refs)):
        build(gi)
        if gi:
            gather(gi - 1)
    gather(len(s_refs) - 1)


def peer_gates(gate, i1, i2, bm):
    t, slots = gate.shape
    n_exp = PEER_KEYS * PEER_KEYS
    spec = pl.BlockSpec((bm, 1, slots), lambda i: (i, 0, 0))
    r3 = lambda a: a.reshape(t, 1, slots)
    return pl.pallas_call(
        _peer_gates_kernel,
        grid=(t // bm,),
        in_specs=[spec, spec, spec],
        out_specs=pl.BlockSpec((bm, n_exp), lambda i: (i, 0)),
        out_shape=jax.ShapeDtypeStruct((t, n_exp), BF16),
        scratch_shapes=[pltpu.VMEM((G_COPY_GROUP * G_PITCH, PEER_KEYS), F32)] * (bm // G_COPY_GROUP),
        compiler_params=_params("parallel"),
        name="peer_gates",
    )(r3(gate), r3(i1), r3(i2))


EXPERT_CHUNKS = 8
OUT_CHUNKS = 4


def _gated_gelu(a, half_gates):
    return (a * (1.0 + lax.erf(a * (2.0 ** -0.5)))).astype(BF16) * half_gates


def _experts_step(hn_ref, gm_ref, ut_ref, v_ref, o_ref):
    a = jnp.dot(hn_ref[...], ut_ref[...], preferred_element_type=F32)
    p = _gated_gelu(a, gm_ref[...])
    o_ref[...] += jnp.dot(p, v_ref[...], preferred_element_type=F32)


def _peer_expert_kernel(x_ref, hn_ref, gm_ref, ut_ref, v_ref, o_ref):
    @pl.when(pl.program_id(1) == 0)
    def _():
        o_ref[...] = x_ref[...]

    _experts_step(hn_ref, gm_ref, ut_ref, v_ref, o_ref)


def _peer_fused_kernel(x_ref, hn_ref, gm_ref, ut_ref, v_ref, xr_ref, g_ref, wq_ref, keys_ref,
                       o_ref, hnr_ref, gate_ref, i1_ref, i2_ref, p_ref, s_ref, *scratch):
    j = pl.program_id(1)

    @pl.when(j == 0)
    def _():
        o_ref[...] = x_ref[...]
        hn = _rms(xr_ref[...], g_ref[...]).astype(BF16)
        hnr_ref[...] = hn
        q = jnp.dot(hn, wq_ref[...], preferred_element_type=F32).astype(BF16)
        for h in range(PEER_HEADS):
            for p, s in enumerate(_route_scores(h, q[:, h * PEER_DKEY:(h + 1) * PEER_DKEY], keys_ref)):
                s_ref[2 * h + p] = s
        _route_init(scratch[4], scratch[5])

    route = _route_pieces(j, lambda p, lanes: s_ref[2 * j + p, :, lanes], xr_ref.shape[0], *scratch)

    def activation_chunk(c):
        width = ut_ref.shape[1] // EXPERT_CHUNKS
        cols = slice(c * width, (c + 1) * width)
        a = jnp.dot(hn_ref[...], ut_ref[:, cols], preferred_element_type=F32)
        p_ref[:, cols] = _gated_gelu(a, gm_ref[:, cols])

    def output_chunk(c):
        width = o_ref.shape[1] // OUT_CHUNKS
        cols = slice(c * width, (c + 1) * width)
        o_ref[:, cols] += jnp.dot(p_ref[...], v_ref[:, cols], preferred_element_type=F32)

    experts = ([functools.partial(activation_chunk, c) for c in range(EXPERT_CHUNKS)]
               + [functools.partial(output_chunk, c) for c in range(OUT_CHUNKS)])
    _interleave(experts, route)

    @pl.when(j == PEER_HEADS - 1)
    def _():
        gt_ref, i1t_ref, i2t_ref = scratch[-3:]
        gate_ref[...] = gt_ref[...].T
        i1_ref[...] = i1t_ref[...].T
        i2_ref[...] = i2t_ref[...].T


def peer_experts(x, hn, gm, ut, v, bm, bn, route=None):
    rows, d = x.shape
    n_exp = ut.shape[1]
    blk = lambda width: pl.BlockSpec((bm, width), lambda i, j: (i, 0))
    in_specs = [blk(d), blk(d),
                pl.BlockSpec((bm, bn), lambda i, j: (i, j)),
                pl.BlockSpec((d, bn), lambda i, j: (0, j)),
                pl.BlockSpec((bn, d), lambda i, j: (j, 0))]
    out_specs = [blk(d)]
    out_shape = [jax.ShapeDtypeStruct((rows, d), F32)]
    args = [x, hn, gm, ut, v]
    kern, scratch, name = _peer_expert_kernel, [], "peer_experts"
    if route is not None:
        g, wq, keys, x_next = route
        assert n_exp // bn == PEER_HEADS and x_next.shape == x.shape
        slots = PEER_HEADS * PEER_TOPK
        in_specs += [blk(d), pl.BlockSpec((1, d), lambda i, j: (0, 0)),
                     pl.BlockSpec(wq.shape, lambda i, j: (0, 0)),
                     pl.BlockSpec(keys.shape, lambda i, j: (0, 0, 0))]
        out_specs += [blk(d), blk(slots), blk(slots), blk(slots)]
        out_shape += [jax.ShapeDtypeStruct((rows, d), BF16)] + [jax.ShapeDtypeStruct((rows, slots), F32)] * 3
        args += [x_next, g.reshape(1, d), wq, keys]
        scratch = [pltpu.VMEM((bm, bn), BF16),
                   pltpu.VMEM((2 * PEER_HEADS, PEER_KEYS, bm), F32)] + _route_scratch(bm)
        kern, name = _peer_fused_kernel, "peer_experts_route"
    out = pl.pallas_call(
        kern,
        grid=(rows // bm, n_exp // bn),
        in_specs=in_specs,
        out_specs=out_specs,
        out_shape=out_shape,
        scratch_shapes=scratch,
        compiler_params=_params("parallel", "arbitrary"),
        name=name,
    )(*args)
    return out[0] if route is None else out


def _final_norm_kernel(x_ref, g_ref, o_ref):
    o_ref[...] = _rms(x_ref[...], g_ref[...])


def final_norm(x, g, bm):
    t, d = x.shape
    return pl.pallas_call(
        _final_norm_kernel,
        grid=(t // bm,),
        in_specs=[pl.BlockSpec((bm, d), lambda i: (i, 0)), pl.BlockSpec((1, d), lambda i: (0, 0))],
        out_specs=pl.BlockSpec((bm, d), lambda i: (i, 0)),
        out_shape=jax.ShapeDtypeStruct((t, d), F32),
        compiler_params=_params("parallel"),
        name="final_norm",
    )(x, g.reshape(1, d))


def _pad_heads(w, heads, dim, axis):
    shape = w.shape[:axis] + (heads, dim) + w.shape[axis + 1:]
    w = w.reshape(shape)
    pad = [(0, 0)] * w.ndim
    pad[axis + 1] = (0, HEAD_PAD - dim)
    w = jnp.pad(w, pad)
    return w.reshape(w.shape[:axis] + (heads * HEAD_PAD,) + w.shape[axis + 2:])


def _rot_half_cols(w):
    half = w.shape[-1] // 2
    return jnp.concatenate([-w[..., half:], w[..., :half]], axis=-1)


def _pack_w_in(w_in):
    rq, rk, rv, rg, qa, kva, kr, sq, sk, sv, gate = jnp.split(
        w_in, [int(p) for p in np.cumsum(
            [512, 512, 512, 512, MLA_Q_RANK, MLA_KV_RANK, MLA_ROPE,
             SWA_HEADS * SWA_DIM, SWA_KV_HEADS * SWA_DIM, SWA_KV_HEADS * SWA_DIM])], axis=-1)
    place = lambda w: jnp.pad(w, ((0, 0), (0, 0), (MLA_NOPE, HEAD_PAD - MLA_NOPE - MLA_ROPE)))
    cols = [rq, rk, rv, rg, _pad_heads(sq, SWA_HEADS, SWA_DIM, 2), gate, qa,
            _pad_heads(sk, SWA_KV_HEADS, SWA_DIM, 2), _pad_heads(sv, SWA_KV_HEADS, SWA_DIM, 2),
            kva, place(kr), place(_rot_half_cols(kr))]
    out = jnp.concatenate(cols, axis=-1).astype(BF16)
    assert out.shape[-1] == Z_COLS
    return out


def _rope_tables(seq, half):
    inv_freq = jnp.power(ROPE_BASE, -jnp.arange(half, dtype=F32) / half)
    ang = jnp.arange(seq, dtype=F32)[:, None] * inv_freq[None, :]
    return jnp.cos(ang), jnp.sin(ang)


def _t5_buckets(rel):
    n = np.maximum(rel, 0)
    max_exact = REL_BUCKETS // 2
    large = max_exact + (np.log(np.maximum(n, 1) / max_exact) / np.log(REL_MAX_DIST / max_exact)
                         * (REL_BUCKETS - max_exact)).astype(np.int32)
    large = np.minimum(large, REL_BUCKETS - 1)
    return np.where(n < max_exact, n, large).astype(np.int32)


def _swa_bias(rel_bias):
    w = SWA_WINDOW
    rel = (np.arange(w)[:, None] + w) - np.arange(2 * w)[None, :]
    bucket = _t5_buckets(rel)
    table = rel_bias.astype(F32).T
    bias = sum(jnp.where(bucket[None] == b, table[:, b][:, None, None], 0.0) for b in range(REL_BUCKETS))
    bias = jnp.where(((rel >= 0) & (rel < w))[None], bias, NEG_INF)
    return bias.reshape(SWA_KV_HEADS, (SWA_HEADS // SWA_KV_HEADS) * w, 2 * w)


def _retention_tables():
    c = RET_CHUNK
    log_gamma = np.log(1.0 - np.exp2(-5.0 - np.arange(RET_HEADS, dtype=np.float64)))
    idx = np.arange(c, dtype=np.float64)
    diff = idx[:, None] - idx[None, :]
    intra = np.where(diff >= 0, np.exp(log_gamma[:, None, None] * np.maximum(diff, 0.0)), 0.0)
    qw = np.exp(log_gamma[:, None] * (idx + 1.0)[None, :])
    kw = np.exp(log_gamma[:, None] * (c - 1.0 - idx)[None, :])
    bc = lambda a: jnp.asarray(np.broadcast_to(a[:, :, None], (RET_HEADS, c, c)), F32)
    chunk_decay = tuple(float(v) for v in np.exp(log_gamma * c))
    return jnp.asarray(intra, F32), bc(qw), bc(kw), chunk_decay


def kernel(x, mem, rel_bias, g_mix, w_in, mla_q_norm, w_mla_qb, mla_kv_norm, w_mla_kvb, swa_sinks,
           w_branch_ret, w_branch_mla, w_branch_swa, w_out, g_xattn, g_mem, w_xq, w_xkv, w_xo,
           g_ffn, w_peer_query, peer_sub_keys, peer_u, peer_v, g_final):
    batch, seq, d = x.shape
    depth = w_in.shape[0]
    t = batch * seq
    mem_len = mem.shape[1]
    bm = min(512, seq)
    big = min(1024, seq)
    n_groups = next(n for n in (4, 2, 1) if batch % n == 0)
    gb = batch // n_groups
    group = gb * seq
    peer_bn = PEER_KEYS * PEER_KEYS // PEER_HEADS

    w_in_p = _pack_w_in(w_in)
    qb = w_mla_qb.reshape(depth, MLA_Q_RANK, MLA_HEADS, MLA_NOPE + MLA_ROPE)
    q_nope, q_rope = qb[..., :MLA_NOPE], qb[..., MLA_NOPE:]
    zpad = jnp.zeros(qb.shape[:3] + (HEAD_PAD - MLA_NOPE - MLA_ROPE,), F32)
    flat = lambda w: w.reshape(w.shape[:2] + (MLA_HEADS * HEAD_PAD,)).astype(BF16)
    w_q1 = flat(jnp.concatenate([q_nope, q_rope, zpad], axis=-1))
    w_q2 = flat(jnp.concatenate([jnp.zeros_like(q_nope), _rot_half_cols(q_rope), zpad], axis=-1))
    kvb = w_mla_kvb.reshape(depth, MLA_KV_RANK, MLA_HEADS, MLA_NOPE + MLA_V)
    w_k = flat(jnp.pad(kvb[..., :MLA_NOPE], ((0, 0),) * 3 + ((0, HEAD_PAD - MLA_NOPE),)))
    w_v = flat(jnp.pad(kvb[..., MLA_NOPE:], ((0, 0),) * 3 + ((0, HEAD_PAD - MLA_V),)))
    w_bret = w_branch_ret.astype(BF16)
    w_bmla = w_branch_mla.astype(BF16)
    w_bswa = _pad_heads(w_branch_swa, SWA_HEADS, SWA_DIM, 1).astype(BF16)
    w_out_b = w_out.astype(BF16)
    w_xq_b, w_xkv_b, w_xo_b = w_xq.astype(BF16), w_xkv.astype(BF16), w_xo.astype(BF16)
    w_pq = w_peer_query.astype(BF16)
    half = PEER_DKEY // 2
    keys = jnp.stack([jnp.pad(peer_sub_keys[:, :, 0], ((0, 0),) * 3 + ((0, half),)),
                      jnp.pad(peer_sub_keys[:, :, 1], ((0, 0),) * 3 + ((half, 0),))], axis=2)
    keys = keys.reshape(depth, 2 * PEER_HEADS, PEER_KEYS, PEER_DKEY).astype(BF16)
    peer_ut = jnp.swapaxes(peer_u, 1, 2).astype(BF16)
    peer_vb = peer_v.astype(BF16)

    cos64, sin64 = _rope_tables(seq, RET_DIM // 2)
    ret_cos = jnp.concatenate([cos64, cos64], axis=1)
    ret_sin = jnp.concatenate([-sin64, sin64], axis=1)
    cos16, sin16 = _rope_tables(seq, MLA_ROPE // 2)
    tail = jnp.zeros((seq, HEAD_PAD - MLA_NOPE - MLA_ROPE), F32)
    mla_cos = jnp.concatenate([jnp.ones((seq, MLA_NOPE), F32), cos16, cos16, tail], axis=1)
    mla_sin = jnp.concatenate([jnp.zeros((seq, MLA_NOPE), F32), sin16, sin16, tail], axis=1)
    intra, ret_qw, ret_kw, chunk_decay = _retention_tables()
    swa_bias = _swa_bias(rel_bias)
    grp = SWA_HEADS // SWA_KV_HEADS

    mem2 = mem.reshape(batch * mem_len, d)
    memkv = [norm_matmul(mem2, g_mem[l], w_xkv_b[l], bm=min(512, batch * mem_len), bn=w_xkv_b.shape[2],
                         out_dtype=BF16).reshape(batch, mem_len, -1) for l in range(depth)]
    sinks = [jnp.broadcast_to(swa_sinks[l].reshape(SWA_KV_HEADS, grp, 1, 1), (SWA_KV_HEADS, grp, SWA_WINDOW, 1)
                              ).reshape(SWA_KV_HEADS, grp * SWA_WINDOW, 1) for l in range(depth)]

    def mixers(xg, l, g):
        z = norm_matmul(xg, g_mix[l], w_in_p[l], bm=big, bn=Z_COLS // 3, out_dtype=BF16)
        y_ret = retention(z, gb, seq, ret_cos, ret_sin, intra, ret_qw, ret_kw, chunk_decay,
                          chunks_per_step=min(4, seq // RET_CHUNK))
        q, k, v = mla_proj(z, mla_q_norm[l], w_q1[l], w_q2[l], mla_kv_norm[l], w_k[l], w_v[l],
                           mla_cos, mla_sin, seq, bm)
        y_mla = mla_attention(q, k, v, gb, seq, bq=big)
        y_swa = swa(z, gb, seq, swa_bias, sinks[l], n_win=min(4, seq // SWA_WINDOW))
        return merge_xattn(xg, y_ret, y_mla, y_swa, z, w_bret[l], w_bmla[l], w_bswa[l], w_out_b[l],
                           g_xattn[l], w_xq_b[l], memkv[l][g * gb:(g + 1) * gb], w_xo_b[l], seq, bm,
                           row0=0, rows=group)

    x = x.reshape(t, d)
    xs = [x[g * group:(g + 1) * group] for g in range(n_groups)]
    stages = [(l, g) for l in range(depth) for g in range(n_groups)]
    route_args = lambda l: (g_ffn[l], w_pq[l], keys[l])
    l0, g0 = stages[0]
    xa = mixers(xs[g0], l0, g0)
    hn, gate, i1, i2 = peer_route(xa, *route_args(l0), bm=min(256, seq))
    for s, (l, g) in enumerate(stages):
        overlap = n_groups > 1 and s + 1 < len(stages)
        if overlap:
            ln, gn = stages[s + 1]
            xa_next = mixers(xs[gn], ln, gn)
        gm = peer_gates(gate, i1, i2, bm=min(256, seq))
        if overlap:
            xs[g], hn, gate, i1, i2 = peer_experts(xa, hn, gm, peer_ut[l], peer_vb[l], bm=bm, bn=peer_bn,
                                                   route=route_args(ln) + (xa_next,))
            xa = xa_next
        else:
            xs[g] = peer_experts(xa, hn, gm, peer_ut[l], peer_vb[l], bm=min(big, group), bn=peer_bn)
            if s + 1 < len(stages):
                ln, gn = stages[s + 1]
                xa = mixers(xs[gn], ln, gn)
                hn, gate, i1, i2 = peer_route(xa, *route_args(ln), bm=min(256, seq))
    out = [final_norm(xg, g_final, bm) for xg in xs]
    return jnp.concatenate(out, axis=0).reshape(batch, seq, d)
```
